```python
import jax, jax.numpy as jnp
from jax import lax
import numpy as np

D_MODEL = 2048
BATCH = 8
SEQ = 4096
DEPTH = 4

MIX_WIDTH = D_MODEL // 2
HEAD_DIM = 64
SWA_HEADS = MIX_WIDTH // HEAD_DIM
SWA_KV_HEADS = SWA_HEADS // 4
WINDOW = 128
CONV_CH = MIX_WIDTH
CONV_WIDTH = 31
SB_HEADS = MIX_WIDTH // HEAD_DIM
SB_BLOCK = 128
HGRN_EXPAND = 128
HGRN_HEADS = MIX_WIDTH // HGRN_EXPAND
HGRN_VDIM = MIX_WIDTH // HGRN_HEADS
HGRN_CHUNK = 64
D_FF = ((8 * D_MODEL // 3 + 255) // 256) * 256
PLE_DIM = 256
N_EVEN = (DEPTH + 1) // 2
N_ODD = DEPTH // 2
EPS = 1e-6
NEG_BIG = -1e30
LB_FLOOR = 1e-20
EVEN_IN = SWA_HEADS * HEAD_DIM + 2 * SWA_KV_HEADS * HEAD_DIM + 2 * CONV_CH
ODD_IN = 3 * SB_HEADS * HEAD_DIM + 2 * HGRN_HEADS * HGRN_EXPAND + 2 * MIX_WIDTH

kernel_name = "hybrid_swa_conv_stickbreak_hgrn2_macaron"


def rmsnorm(x, g):
    xf = x.astype(jnp.float32)
    y = xf * lax.rsqrt(jnp.mean(xf * xf, axis=-1, keepdims=True) + EPS)
    return (y * g.astype(jnp.float32)).astype(x.dtype)


def swiglu(h, w_in, w_out):
    a, b = jnp.split(h @ w_in, 2, axis=-1)
    return (jax.nn.silu(a) * b) @ w_out


def swa_with_sinks(q, k, v, sinks):
    B, S, Hq, dh = q.shape
    Hkv = k.shape[2]
    G = Hq // Hkv
    W = WINDOW
    n = S // W
    qb = q.reshape(B, n, W, Hkv, G, dh)
    kb = k.reshape(B, n, W, Hkv, dh)
    vb = v.reshape(B, n, W, Hkv, dh)
    pad = ((0, 0), (1, 0), (0, 0), (0, 0), (0, 0))
    kk = jnp.concatenate([jnp.pad(kb, pad)[:, :-1], kb], axis=2)
    vv = jnp.concatenate([jnp.pad(vb, pad)[:, :-1], vb], axis=2)
    logits = jnp.einsum('bnqhgd,bnshd->bnhgqs', qb, kk).astype(jnp.float32) * (dh ** -0.5)
    qi = jnp.arange(W)[:, None]
    sj = jnp.arange(2 * W)[None, :]
    rel = qi + W - sj
    band = (rel >= 0) & (rel < W)
    blk = jnp.arange(n)[:, None, None]
    mask = band[None] & ((blk > 0) | (sj >= W)[None])
    logits = jnp.where(mask[None, :, None, None], logits, NEG_BIG)
    sink = sinks.astype(jnp.float32).reshape(1, 1, Hkv, G, 1, 1)
    m = jnp.maximum(jnp.max(logits, axis=-1, keepdims=True), sink)
    e = jnp.exp(logits - m)
    probs = e / (jnp.sum(e, axis=-1, keepdims=True) + jnp.exp(sink - m))
    out = jnp.einsum('bnhgqs,bnshd->bnqhgd', probs.astype(v.dtype), vv)
    return out.reshape(B, S, Hq * dh)


def conformer_conv(u, conv_w, conv_b, ln_g, ln_b):
    a, gate = jnp.split(u, 2, axis=-1)
    h = a * jax.nn.sigmoid(gate)
    C = h.shape[-1]
    h = lax.conv_general_dilated(
        h, conv_w[:, None, :], window_strides=(1,), padding=[(CONV_WIDTH - 1, 0)],
        dimension_numbers=('NWC', 'WIO', 'NWC'), feature_group_count=C) + conv_b
    hf = h.astype(jnp.float32)
    mu = jnp.mean(hf, axis=-1, keepdims=True)
    var = jnp.mean(jnp.square(hf - mu), axis=-1, keepdims=True)
    hf = (hf - mu) * lax.rsqrt(var + EPS) * ln_g.astype(jnp.float32) + ln_b.astype(jnp.float32)
    return jax.nn.silu(hf).astype(u.dtype)


def stick_breaking_attn(q, k, v):
    B, S, H, dh = q.shape
    n = S // SB_BLOCK
    qb = q.reshape(B, n, SB_BLOCK, H, dh).transpose(1, 0, 3, 2, 4)
    kh = k.transpose(0, 2, 1, 3)
    vh = v.transpose(0, 2, 1, 3)
    kpos = jnp.arange(S)

    def block(args):
        qblk, start = args
        z = jnp.einsum('bhqd,bhsd->bhqs', qblk, kh).astype(jnp.float32) * (dh ** -0.5)
        qpos = start + jnp.arange(SB_BLOCK)
        causal = kpos[None, :] < qpos[:, None]
        log_beta = jax.nn.log_sigmoid(z)
        log_1m = jnp.where(causal, jax.nn.log_sigmoid(-z), 0.0)
        tail = lax.cumsum(log_1m, axis=3, reverse=True) - log_1m
        A = jnp.where(causal, jnp.exp(log_beta + tail), 0.0)
        return jnp.einsum('bhqs,bhsd->bhqd', A.astype(vh.dtype), vh)

    out = lax.map(block, (qb, jnp.arange(n, dtype=jnp.int32) * SB_BLOCK))
    return out.transpose(1, 0, 3, 2, 4).reshape(B, S, H * dh)


def hgrn2_chunked(q, k, v, logf):
    B, S, H, dk = q.shape
    dv = v.shape[-1]
    nC = S // HGRN_CHUNK

    def to_chunks(t):
        return t.astype(jnp.float32).reshape(B, nC, HGRN_CHUNK, H, t.shape[-1]).transpose(1, 0, 3, 2, 4)

    causal = jnp.tril(jnp.ones((HGRN_CHUNK, HGRN_CHUNK), dtype=bool))

    def step(state, inp):
        qc, kc, vc, gc = inp
        b = jnp.cumsum(gc, axis=2)
        o_inter = jnp.einsum('bhtk,bhkv->bhtv', qc * jnp.exp(b), state)
        diff = b[:, :, :, None, :] - b[:, :, None, :, :]
        decay = jnp.exp(jnp.where(causal[:, :, None], diff, NEG_BIG))
        scores = jnp.einsum('bhtk,bhtsk,bhsk->bhts', qc, decay, kc)
        o_intra = jnp.einsum('bhts,bhsv->bhtv', scores, vc)
        b_last = b[:, :, -1:, :]
        new_state = jnp.exp(b_last[:, :, 0, :])[..., None] * state + \
            jnp.einsum('bhsk,bhsv->bhkv', kc * jnp.exp(b_last - b), vc)
        return new_state, o_inter + o_intra

    s0 = jnp.zeros((B, H, dk, dv), jnp.float32)
    _, o = lax.scan(step, s0, (to_chunks(q), to_chunks(k), to_chunks(v), to_chunks(logf)))
    return o.transpose(1, 0, 3, 2, 4).reshape(B, S, H, dv)


def even_mixer(h, w_in, w_out, sinks, conv_w, conv_b, ln_g, ln_b):
    B, S, _ = h.shape
    u = h @ w_in
    nq = SWA_HEADS * HEAD_DIM
    nkv = SWA_KV_HEADS * HEAD_DIM
    q, k, v, c = jnp.split(u, [nq, nq + nkv, nq + 2 * nkv], axis=-1)
    a_out = swa_with_sinks(q.reshape(B, S, SWA_HEADS, HEAD_DIM),
                           k.reshape(B, S, SWA_KV_HEADS, HEAD_DIM),
                           v.reshape(B, S, SWA_KV_HEADS, HEAD_DIM), sinks)
    b_out = conformer_conv(c, conv_w, conv_b, ln_g, ln_b)
    return jnp.concatenate([a_out, b_out], axis=-1) @ w_out


def odd_mixer(h, w_in, w_out, lb, norm_g):
    B, S, _ = h.shape
    u = h @ w_in
    w = SB_HEADS * HEAD_DIM
    fd = HGRN_HEADS * HGRN_EXPAND
    qc, kc, vc, qd, fz, iv, g = jnp.split(
        u, [w, 2 * w, 3 * w, 3 * w + fd, 3 * w + 2 * fd, 3 * w + 2 * fd + MIX_WIDTH], axis=-1)
    c_out = stick_breaking_attn(qc.reshape(B, S, SB_HEADS, HEAD_DIM),
                                kc.reshape(B, S, SB_HEADS, HEAD_DIM),
                                vc.reshape(B, S, SB_HEADS, HEAD_DIM))
    fz32 = fz.astype(jnp.float32)
    logf = jnp.logaddexp(jnp.log(jnp.maximum(lb, LB_FLOOR)),
                         jnp.log1p(-lb) + jax.nn.log_sigmoid(fz32))
    kd = (1.0 - lb) * jax.nn.sigmoid(-fz32)
    o = hgrn2_chunked(jax.nn.silu(qd).reshape(B, S, HGRN_HEADS, HGRN_EXPAND),
                      kd.reshape(B, S, HGRN_HEADS, HGRN_EXPAND),
                      iv.reshape(B, S, HGRN_HEADS, HGRN_VDIM),
                      logf.reshape(B, S, HGRN_HEADS, HGRN_EXPAND))
    o = o * lax.rsqrt(jnp.mean(o * o, axis=-1, keepdims=True) + EPS)
    o = o * norm_g.astype(jnp.float32).reshape(HGRN_HEADS, HGRN_VDIM)
    d_out = (o.reshape(B, S, MIX_WIDTH) * jax.nn.silu(g.astype(jnp.float32))).astype(h.dtype)
    return jnp.concatenate([c_out, d_out], axis=-1) @ w_out


def setup_inputs(seed: int = 0) -> dict:
    key = jax.random.key(seed)
    ks = jax.random.split(key, 24)

    def nrm(k, shape, scale):
        return jax.random.normal(k, shape, jnp.float32) * scale

    def gain(k, shape):
        return 1.0 + nrm(k, shape, 0.02)

    return {
        "x": nrm(ks[0], (BATCH, SEQ, D_MODEL), 1.0),
        "p": nrm(ks[1], (DEPTH, BATCH, SEQ, PLE_DIM), 1.0),
        "ffn_norm": gain(ks[2], (DEPTH, 2, D_MODEL)),
        "ffn_w_in": nrm(ks[3], (DEPTH, 2, D_MODEL, 2 * D_FF), D_MODEL ** -0.5),
        "ffn_w_out": nrm(ks[4], (DEPTH, 2, D_FF, D_MODEL), D_FF ** -0.5),
        "mix_norm": gain(ks[5], (DEPTH, D_MODEL)),
        "even_w_in": nrm(ks[6], (N_EVEN, D_MODEL, EVEN_IN), D_MODEL ** -0.5),
        "even_w_out": nrm(ks[7], (N_EVEN, 2 * MIX_WIDTH, D_MODEL), (2 * MIX_WIDTH) ** -0.5),
        "swa_sinks": nrm(ks[8], (N_EVEN, SWA_HEADS), 0.5),
        "conv_w": nrm(ks[9], (N_EVEN, CONV_WIDTH, CONV_CH), CONV_WIDTH ** -0.5),
        "conv_b": nrm(ks[10], (N_EVEN, CONV_CH), 0.02),
        "conv_ln_g": gain(ks[11], (N_EVEN, CONV_CH)),
        "conv_ln_b": nrm(ks[12], (N_EVEN, CONV_CH), 0.02),
        "odd_w_in": nrm(ks[13], (N_ODD, D_MODEL, ODD_IN), D_MODEL ** -0.5),
        "odd_w_out": nrm(ks[14], (N_ODD, 2 * MIX_WIDTH, D_MODEL), (2 * MIX_WIDTH) ** -0.5),
        "hgrn_lb_logits": nrm(ks[15], (N_ODD, HGRN_HEADS * HGRN_EXPAND), 0.1),
        "hgrn_norm": gain(ks[16], (N_ODD, MIX_WIDTH)),
        "ple_norm": gain(ks[17], (DEPTH, D_MODEL)),
        "ple_w_gate": nrm(ks[18], (DEPTH, D_MODEL, D_MODEL), D_MODEL ** -0.5),
        "ple_w_proj": nrm(ks[19], (DEPTH, PLE_DIM, D_MODEL), PLE_DIM ** -0.5),
        "final_norm": gain(ks[20], (D_MODEL,)),
    }


def reference(x, p, ffn_norm, ffn_w_in, ffn_w_out, mix_norm, even_w_in, even_w_out,
              swa_sinks, conv_w, conv_b, conv_ln_g, conv_ln_b, odd_w_in, odd_w_out,
              hgrn_lb_logits, hgrn_norm, ple_norm, ple_w_gate, ple_w_proj, final_norm):
    lb_sm = jax.nn.softmax(hgrn_lb_logits.astype(jnp.float32), axis=0)
    lower_bounds = jnp.cumsum(lb_sm, axis=0) - lb_sm[0]
    for i in range(DEPTH):
        x = x + 0.5 * swiglu(rmsnorm(x, ffn_norm[i, 0]), ffn_w_in[i, 0], ffn_w_out[i, 0])
        h = rmsnorm(x, mix_norm[i])
        j = i // 2
        if i % 2 == 0:
            x = x + even_mixer(h, even_w_in[j], even_w_out[j], swa_sinks[j], conv_w[j],
                               conv_b[j], conv_ln_g[j], conv_ln_b[j])
        else:
            x = x + odd_mixer(h, odd_w_in[j], odd_w_out[j], lower_bounds[j], hgrn_norm[j])
        x = x + 0.5 * swiglu(rmsnorm(x, ffn_norm[i, 1]), ffn_w_in[i, 1], ffn_w_out[i, 1])
        gate = jax.nn.sigmoid(rmsnorm(x, ple_norm[i]) @ ple_w_gate[i])
        x = x + gate * (p[i] @ ple_w_proj[i])
    return rmsnorm(x, final_norm)
```

```python
import functools

import jax
import jax.numpy as jnp
from jax import lax
from jax.experimental import pallas as pl
from jax.experimental.pallas import tpu as pltpu

F32 = jnp.float32
BF16 = jnp.bfloat16

EPS = 1e-6
NEG_BIG = -1e30
LB_FLOOR = 1e-20
HEAD_DIM = 64
WINDOW = 128
CONV_WIDTH = 31
HGRN_DK = 128
LANES = 128
HGRN_SUB = 16
HGRN_ROWS = 256
SB_BLOCK = 128
SB_EXIT = -110.0
VMEM_LIMIT = 52 * 1024 * 1024


def _cparams(sem):
    return pltpu.CompilerParams(dimension_semantics=sem, vmem_limit_bytes=VMEM_LIMIT)


def _rms(xf, g):
    ms = jnp.mean(xf * xf, axis=-1, keepdims=True)
    return xf * lax.rsqrt(ms + EPS) * g


def _sigmoid(a):
    return 1.0 / (1.0 + jnp.exp(-a))


def _ffn_kernel(x_ref, g_ref, wa_ref, wb_ref, wo_ref, o_ref, h_ref):
    @pl.when(pl.program_id(1) == 0)
    def _():
        xf = x_ref[...]
        h_ref[...] = _rms(xf, g_ref[...]).astype(BF16)
        o_ref[...] = xf

    h = h_ref[...]
    a = jnp.dot(h, wa_ref[...], preferred_element_type=F32)
    b = jnp.dot(h, wb_ref[...], preferred_element_type=F32)
    gg = (0.5 * (a * _sigmoid(a)) * b).astype(BF16)
    o_ref[...] += jnp.dot(gg, wo_ref[...], preferred_element_type=F32)


def _ffn(x, g, w_in, w_out, bm, bn):
    m, d = x.shape
    f = w_out.shape[0]
    nj = f // bn
    return pl.pallas_call(
        _ffn_kernel,
        grid=(m // bm, nj),
        in_specs=[
            pl.BlockSpec((bm, d), lambda i, j: (i, 0)),
            pl.BlockSpec((1, d), lambda i, j: (0, 0)),
            pl.BlockSpec((d, bn), lambda i, j: (0, j)),
            pl.BlockSpec((d, bn), lambda i, j: (0, j + nj)),
            pl.BlockSpec((bn, d), lambda i, j: (j, 0)),
        ],
        out_specs=pl.BlockSpec((bm, d), lambda i, j: (i, 0)),
        out_shape=jax.ShapeDtypeStruct((m, d), F32),
        scratch_shapes=[pltpu.VMEM((bm, d), BF16)],
        compiler_params=_cparams(("parallel", "arbitrary")),
        name="ffn",
    )(x, g, w_in, w_in, w_out)


def _norm_proj_kernel(x_ref, g_ref, w_ref, o_ref, h_ref):
    @pl.when(pl.program_id(1) == 0)
    def _():
        h_ref[...] = _rms(x_ref[...], g_ref[...]).astype(BF16)

    o_ref[...] = jnp.dot(h_ref[...], w_ref[...], preferred_element_type=F32).astype(o_ref.dtype)


def _norm_proj(x, g, w, out_dtype, bm, bn):
    m, d = x.shape
    n = w.shape[1]
    return pl.pallas_call(
        _norm_proj_kernel,
        grid=(m // bm, n // bn),
        in_specs=[
            pl.BlockSpec((bm, d), lambda i, j: (i, 0)),
            pl.BlockSpec((1, d), lambda i, j: (0, 0)),
            pl.BlockSpec((d, bn), lambda i, j: (0, j)),
        ],
        out_specs=pl.BlockSpec((bm, bn), lambda i, j: (i, j)),
        out_shape=jax.ShapeDtypeStruct((m, n), out_dtype),
        scratch_shapes=[pltpu.VMEM((bm, d), BF16)],
        compiler_params=_cparams(("parallel", "arbitrary")),
        name="norm_proj",
    )(x, g, w)


def _norm_glu_kernel(x_ref, g_ref, wa_ref, wb_ref, o_ref, h_ref):
    @pl.when(pl.program_id(1) == 0)
    def _():
        h_ref[...] = _rms(x_ref[...], g_ref[...]).astype(BF16)

    h = h_ref[...]
    a = jnp.dot(h, wa_ref[...], preferred_element_type=F32)
    b = jnp.dot(h, wb_ref[...], preferred_element_type=F32)
    o_ref[...] = a * _sigmoid(b)


def _norm_glu(x, g, w, bm, bn):
    m, d = x.shape
    c = w.shape[1] // 2
    nj = c // bn
    return pl.pallas_call(
        _norm_glu_kernel,
        grid=(m // bm, nj),
        in_specs=[
            pl.BlockSpec((bm, d), lambda i, j: (i, 0)),
            pl.BlockSpec((1, d), lambda i, j: (0, 0)),
            pl.BlockSpec((d, bn), lambda i, j: (0, j)),
            pl.BlockSpec((d, bn), lambda i, j: (0, j + nj)),
        ],
        out_specs=pl.BlockSpec((bm, bn), lambda i, j: (i, j)),
        out_shape=jax.ShapeDtypeStruct((m, c), F32),
        scratch_shapes=[pltpu.VMEM((bm, d), BF16)],
        compiler_params=_cparams(("parallel", "arbitrary")),
        name="norm_glu",
    )(x, g, w, w)


def _out_proj_kernel(x_ref, a_ref, b_ref, wa_ref, wb_ref, o_ref):
    acc = jnp.dot(a_ref[...], wa_ref[...], preferred_element_type=F32)
    acc += jnp.dot(b_ref[...], wb_ref[...], preferred_element_type=F32)
    o_ref[...] = x_ref[...] + acc


def _out_proj(x, a, b, w, bm, bn):
    m, d = x.shape
    ka = a.shape[1]
    nka = ka // bn if ka % bn == 0 else None
    assert ka == b.shape[1] and w.shape[0] == 2 * ka
    return pl.pallas_call(
        _out_proj_kernel,
        grid=(m // bm, d // bn),
        in_specs=[
            pl.BlockSpec((bm, bn), lambda i, j: (i, j)),
            pl.BlockSpec((bm, ka), lambda i, j: (i, 0)),
            pl.BlockSpec((bm, ka), lambda i, j: (i, 0)),
            pl.BlockSpec((ka, bn), lambda i, j: (0, j)),
            pl.BlockSpec((ka, bn), lambda i, j: (1, j)),
        ],
        out_specs=pl.BlockSpec((bm, bn), lambda i, j: (i, j)),
        out_shape=jax.ShapeDtypeStruct((m, d), F32),
        compiler_params=_cparams(("parallel", "arbitrary")),
        name="out_proj",
    )(x, a, b, w, w)


def _ple_kernel(x_ref, g_ref, wg_ref, p_ref, wp_ref, o_ref, h_ref, pb_ref, *, bn):
    j = pl.program_id(1)

    @pl.when(j == 0)
    def _():
        h_ref[...] = _rms(x_ref[...], g_ref[...]).astype(BF16)
        pb_ref[...] = p_ref[...].astype(BF16)

    gate = _sigmoid(jnp.dot(h_ref[...], wg_ref[...], preferred_element_type=F32))
    proj = jnp.dot(pb_ref[...], wp_ref[...], preferred_element_type=F32)
    xs = x_ref[:, pl.ds(pl.multiple_of(j * bn, bn), bn)]
    o_ref[...] = xs + gate * proj


def _ple(x, g, wg, p, wp, bm, bn):
    m, d = x.shape
    e = p.shape[1]
    return pl.pallas_call(
        functools.partial(_ple_kernel, bn=bn),
        grid=(m // bm, d // bn),
        in_specs=[
            pl.BlockSpec((bm, d), lambda i, j: (i, 0)),
            pl.BlockSpec((1, d), lambda i, j: (0, 0)),
            pl.BlockSpec((d, bn), lambda i, j: (0, j)),
            pl.BlockSpec((bm, e), lambda i, j: (i, 0)),
            pl.BlockSpec((e, bn), lambda i, j: (0, j)),
        ],
        out_specs=pl.BlockSpec((bm, bn), lambda i, j: (i, j)),
        out_shape=jax.ShapeDtypeStruct((m, d), F32),
        scratch_shapes=[pltpu.VMEM((bm, d), BF16), pltpu.VMEM((bm, e), BF16)],
        compiler_params=_cparams(("parallel", "arbitrary")),
        name="ple",
    )(x, g, wg, p, wp)


def _final_norm_kernel(x_ref, g_ref, o_ref):
    o_ref[...] = _rms(x_ref[...], g_ref[...])


def _final_norm(x, g, bm):
    m, d = x.shape
    return pl.pallas_call(
        _final_norm_kernel,
        grid=(m // bm,),
        in_specs=[pl.BlockSpec((bm, d), lambda i: (i, 0)), pl.BlockSpec((1, d), lambda i: (0, 0))],
        out_specs=pl.BlockSpec((bm, d), lambda i: (i, 0)),
        out_shape=jax.ShapeDtypeStruct((m, d), F32),
        compiler_params=_cparams(("parallel",)),
        name="final_norm",
    )(x, g)


def _swa_kernel(sink_ref, q_ref, kc_ref, kp_ref, vc_ref, vp_ref, o_ref, *, n_pairs, group):
    w = WINDOW
    n = pl.program_id(1)
    qi = lax.broadcasted_iota(jnp.int32, (w, 2 * w), 0)
    sj = lax.broadcasted_iota(jnp.int32, (w, 2 * w), 1)
    rel = qi + w - sj
    first_key = jnp.where(n > 0, 0, w)
    valid = (rel >= 0) & (rel < w) & (sj >= first_key)
    lo = lax.broadcasted_iota(jnp.int32, (1, LANES), 1) < HEAD_DIM
    scale = HEAD_DIM ** -0.5
    for p in range(n_pairs):
        kv = (2 * p) // group
        cs = slice(kv * LANES, (kv + 1) * LANES)
        qp = q_ref[0, :, p * LANES:(p + 1) * LANES]
        kd = jnp.concatenate([kp_ref[0, :, cs], kc_ref[0, :, cs]], axis=0)
        vd = jnp.concatenate([vp_ref[0, :, cs], vc_ref[0, :, cs]], axis=0)
        acc = jnp.zeros((w, LANES), F32)
        for half in range(2):
            sel = lo if half == 0 else jnp.logical_not(lo)
            qm = jnp.where(sel, qp, jnp.zeros_like(qp))
            logits = lax.dot_general(qm, kd, (((1,), (1,)), ((), ())), preferred_element_type=F32) * scale
            logits = jnp.where(valid, logits, NEG_BIG)
            sink = sink_ref[2 * p + half]
            mx = jnp.maximum(jnp.max(logits, axis=-1, keepdims=True), sink)
            e = jnp.exp(logits - mx)
            probs = e / (jnp.sum(e, axis=-1, keepdims=True) + jnp.exp(sink - mx))
            vm = jnp.where(sel, vd, jnp.zeros_like(vd))
            acc += jnp.dot(probs.astype(BF16), vm, preferred_element_type=F32)
        o_ref[0, :, p * LANES:(p + 1) * LANES] = acc.astype(o_ref.dtype)


def _swa(qkv, sinks, n_q_heads, n_kv_heads):
    bsz, s, _ = qkv.shape
    w = WINDOW
    nq = n_q_heads * HEAD_DIM
    nk = n_kv_heads * LANES
    assert nq % nk == 0
    kblk = nq // nk
    prev = lambda b, n: jnp.maximum(n - 1, 0)
    return pl.pallas_call(
        functools.partial(_swa_kernel, n_pairs=n_q_heads // 2, group=n_q_heads // n_kv_heads),
        grid=(bsz, s // w),
        in_specs=[
            pl.BlockSpec(memory_space=pltpu.SMEM),
            pl.BlockSpec((1, w, nq), lambda b, n: (b, n, 0)),
            pl.BlockSpec((1, w, nk), lambda b, n: (b, n, kblk)),
            pl.BlockSpec((1, w, nk), lambda b, n: (b, prev(b, n), kblk)),
            pl.BlockSpec((1, w, nk), lambda b, n: (b, n, kblk + 1)),
            pl.BlockSpec((1, w, nk), lambda b, n: (b, prev(b, n), kblk + 1)),
        ],
        out_specs=pl.BlockSpec((1, w, nq), lambda b, n: (b, n, 0)),
        out_shape=jax.ShapeDtypeStruct((bsz, s, nq), BF16),
        compiler_params=_cparams(("parallel", "arbitrary")),
        name="swa",
    )(sinks, qkv, qkv, qkv, qkv, qkv)


def _conv_kernel(cur_ref, halo_ref, w_ref, b_ref, lg_ref, lb_ref, o_ref, slab_ref, *, bs, halo, rows):
    i = pl.program_id(1)
    hv = halo_ref[0]
    slab_ref[0:halo, :] = jnp.where(i > 0, hv, jnp.zeros_like(hv))
    slab_ref[halo:, :] = cur_ref[0]
    off = halo - (CONV_WIDTH - 1)
    for r in range(bs // rows):
        acc = jnp.zeros((rows, cur_ref.shape[2]), F32) + b_ref[...]
        for t in range(CONV_WIDTH):
            acc += slab_ref[r * rows + off + t: r * rows + off + t + rows, :] * w_ref[t:t + 1, :]
        mu = jnp.mean(acc, axis=-1, keepdims=True)
        cen = acc - mu
        var = jnp.mean(cen * cen, axis=-1, keepdims=True)
        y = cen * lax.rsqrt(var + EPS) * lg_ref[...] + lb_ref[...]
        o_ref[0, r * rows:(r + 1) * rows, :] = (y * _sigmoid(y)).astype(o_ref.dtype)


def _conv(c, conv_w, conv_b, ln_g, ln_b, bs):
    bsz, s, ch = c.shape
    halo = 32
    rows = 16
    return pl.pallas_call(
        functools.partial(_conv_kernel, bs=bs, halo=halo, rows=rows),
        grid=(bsz, s // bs),
        in_specs=[
            pl.BlockSpec((1, bs, ch), lambda b, i: (b, i, 0)),
            pl.BlockSpec((1, halo, ch), lambda b, i: (b, jnp.maximum(i * (bs // halo) - 1, 0), 0)),
            pl.BlockSpec((CONV_WIDTH, ch), lambda b, i: (0, 0)),
            pl.BlockSpec((1, ch), lambda b, i: (0, 0)),
            pl.BlockSpec((1, ch), lambda b, i: (0, 0)),
            pl.BlockSpec((1, ch), lambda b, i: (0, 0)),
        ],
        out_specs=pl.BlockSpec((1, bs, ch), lambda b, i: (b, i, 0)),
        out_shape=jax.ShapeDtypeStruct((bsz, s, ch), BF16),
        scratch_shapes=[pltpu.VMEM((bs + halo, ch), F32)],
        compiler_params=_cparams(("parallel", "arbitrary")),
        name="conv",
    )(c, c, conv_w, conv_b, ln_g, ln_b)


def _sb_kernel(q_ref, k_ref, v_ref, o_ref):
    blk = SB_BLOCK
    qb = pl.program_id(2)
    lo = lax.broadcasted_iota(jnp.int32, (1, LANES), 1) < HEAD_DIM
    qp = q_ref[0]
    zero_q = jnp.zeros_like(qp)
    qh = (jnp.where(lo, qp, zero_q), jnp.where(lo, zero_q, qp))
    row = lax.broadcasted_iota(jnp.int32, (blk, blk), 0)
    col = lax.broadcasted_iota(jnp.int32, (blk, blk), 1)
    tri = jnp.where(row > col, 1.0, 0.0).astype(BF16)
    causal = col < row
    scale = HEAD_DIM ** -0.5

    def block(kb, carries, acc, diag):
        start = pl.multiple_of(kb * blk, blk)
        ks = k_ref[0, pl.ds(start, blk), :]
        vs = v_ref[0, pl.ds(start, blk), :]
        zero_v = jnp.zeros_like(vs)
        vh = (jnp.where(lo, vs, zero_v), jnp.where(lo, zero_v, vs))
        new_carries = []
        for h in range(2):
            z = lax.dot_general(qh[h], ks, (((1,), (1,)), ((), ())), preferred_element_type=F32) * scale
            sp = jnp.log1p(jnp.exp(-jnp.abs(z)))
            log_beta = jnp.minimum(z, 0.0) - sp
            log_1m = -(jnp.maximum(z, 0.0) + sp)
            if diag:
                log_1m = jnp.where(causal, log_1m, 0.0)
            hi = log_1m.astype(BF16)
            rest = (log_1m - hi.astype(F32)).astype(BF16)
            tail = (jnp.dot(hi, tri, preferred_element_type=F32)
                    + jnp.dot(rest, tri, preferred_element_type=F32) + carries[h])
            a = jnp.exp(log_beta + tail)
            if diag:
                a = jnp.where(causal, a, 0.0)
            acc = acc + jnp.dot(a.astype(BF16), vh[h], preferred_element_type=F32)
            new_carries.append(carries[h] + jnp.sum(log_1m, axis=1, keepdims=True))
        return tuple(new_carries), acc

    zc = jnp.zeros((blk, 1), F32)
    carries, acc = block(qb, (zc, zc), jnp.zeros((blk, LANES), F32), True)

    def cond(st):
        kb, c0, c1, _ = st
        return jnp.logical_and(kb >= 0, jnp.maximum(jnp.max(c0), jnp.max(c1)) > SB_EXIT)

    def body(st):
        kb, c0, c1, acc = st
        (c0, c1), acc = block(kb, (c0, c1), acc, False)
        return kb - 1, c0, c1, acc

    _, _, _, acc = lax.while_loop(cond, body, (qb - 1, carries[0], carries[1], acc))
    o_ref[0] = acc.astype(o_ref.dtype)


def _sb_attn(qkv, n_heads):
    bsz, s, _ = qkv.shape
    npair = n_heads * HEAD_DIM // LANES
    blk = SB_BLOCK
    return pl.pallas_call(
        _sb_kernel,
        grid=(bsz, npair, s // blk),
        in_specs=[
            pl.BlockSpec((1, blk, LANES), lambda b, p, i: (b, i, p)),
            pl.BlockSpec((1, s, LANES), lambda b, p, i: (b, 0, npair + p)),
            pl.BlockSpec((1, s, LANES), lambda b, p, i: (b, 0, 2 * npair + p)),
        ],
        out_specs=pl.BlockSpec((1, blk, LANES), lambda b, p, i: (b, i, p)),
        out_shape=jax.ShapeDtypeStruct((bsz, s, npair * LANES), BF16),
        compiler_params=_cparams(("parallel", "parallel", "arbitrary")),
        name="sb_attn",
    )(qkv, qkv, qkv)


def _hgrn_kernel(lbl_ref, qd_ref, fz_ref, iv_ref, gt_ref, ng_ref, sel_ref, o_ref,
                 state_ref, q_s, k_s, b_s, v_s, oi_s, pcat_s, *, layer, rows):
    sub = HGRN_SUB
    nsub = rows // sub

    @pl.when(pl.program_id(2) == 0)
    def _():
        state_ref[...] = jnp.zeros_like(state_ref)

    logits = lbl_ref[...]
    ex = jnp.exp(logits - jnp.max(logits, axis=0, keepdims=True))
    sm = ex / jnp.sum(ex, axis=0, keepdims=True)
    cum = sm[0:1, :]
    for r in range(1, layer + 1):
        cum = cum + sm[r:r + 1, :]
    lb = cum - sm[0:1, :]

    fz = fz_ref[0]
    e = jnp.exp(-jnp.abs(fz))
    sp = jnp.log1p(e)
    log_sig = jnp.minimum(fz, 0.0) - sp
    la = jnp.log(jnp.maximum(lb, LB_FLOOR))
    lc = jnp.log1p(-lb) + log_sig
    logf = jnp.maximum(la, lc) + jnp.log1p(jnp.exp(-jnp.abs(la - lc)))
    k_s[...] = (1.0 - lb) * (jnp.where(fz >= 0.0, e, 1.0) / (1.0 + e))
    qd = qd_ref[0]
    q_s[...] = qd * _sigmoid(qd)
    v_s[...] = iv_ref[0].astype(BF16)

    ri = lax.broadcasted_iota(jnp.int32, (rows, rows), 0)
    ci = lax.broadcasted_iota(jnp.int32, (rows, rows), 1)
    same = (ri // sub) == (ci // sub)
    cum_mat = jnp.where(same & (ci <= ri), 1.0, 0.0).astype(F32)
    b_s[...] = jnp.dot(cum_mat, logf, preferred_element_type=F32, precision=lax.Precision.HIGHEST)

    trow = lax.broadcasted_iota(jnp.int32, (sub, LANES), 0)

    def diag_body(i, carry):
        base = pl.multiple_of(i * sub, sub)
        bi = b_s[pl.ds(base, sub), :]
        qi = q_s[pl.ds(base, sub), :]
        for s in range(sub):
            brow = b_s[pl.ds(base + s, 1), :]
            krow = k_s[pl.ds(base + s, 1), :]
            dec = jnp.exp(jnp.where(trow >= s, bi - brow, NEG_BIG))
            pcat_s[pl.ds(base, sub), s * LANES:(s + 1) * LANES] = (qi * krow * dec).astype(BF16)
        return carry

    lax.fori_loop(0, nsub, diag_body, 0)
    s_rep = jnp.dot(pcat_s[...], sel_ref[...], preferred_element_type=F32)
    s_bd = jnp.where(same, s_rep, 0.0).astype(BF16)
    oi_s[...] = jnp.dot(s_bd, v_s[...], preferred_element_type=F32)

    def chunk_body(c, carry):
        base = pl.multiple_of(c * sub, sub)
        bc = b_s[pl.ds(base, sub), :]
        blast = b_s[pl.ds(base + sub - 1, 1), :]
        st = state_ref[...]
        qe = (q_s[pl.ds(base, sub), :] * jnp.exp(bc)).astype(BF16)
        o_inter = lax.dot_general(qe, st.astype(BF16), (((1,), (1,)), ((), ())), preferred_element_type=F32)
        ke = (k_s[pl.ds(base, sub), :] * jnp.exp(blast - bc)).astype(BF16)
        vc = v_s[pl.ds(base, sub), :]
        kv = lax.dot_general(vc, ke, (((0,), (0,)), ((), ())), preferred_element_type=F32)
        state_ref[...] = st * jnp.exp(blast) + kv
        oi_s[pl.ds(base, sub), :] += o_inter
        return carry

    lax.fori_loop(0, nsub, chunk_body, 0)

    o = oi_s[...]
    o = o * lax.rsqrt(jnp.mean(o * o, axis=-1, keepdims=True) + EPS) * ng_ref[...]
    gt = gt_ref[0]
    o_ref[0] = (o * (gt * _sigmoid(gt))).astype(o_ref.dtype)


def _hgrn(hg, lb_logits, norm_g, layer, n_heads):
    bsz, s, _ = hg.shape
    rows = HGRN_ROWS
    dk = HGRN_DK
    nl = lb_logits.shape[0]
    kk = lax.broadcasted_iota(jnp.int32, (HGRN_SUB * dk, rows), 0) // dk
    cc = lax.broadcasted_iota(jnp.int32, (HGRN_SUB * dk, rows), 1) % HGRN_SUB
    sel = (kk == cc).astype(BF16)
    blk = lambda off: pl.BlockSpec((1, rows, dk), lambda b, h, r: (b, r, off * n_heads + h))
    return pl.pallas_call(
        functools.partial(_hgrn_kernel, layer=layer, rows=rows),
        grid=(bsz, n_heads, s // rows),
        in_specs=[
            pl.BlockSpec((nl, dk), lambda b, h, r: (0, h)),
            blk(0), blk(1), blk(2), blk(3),
            pl.BlockSpec((1, dk), lambda b, h, r: (0, h)),
            pl.BlockSpec((HGRN_SUB * dk, rows), lambda b, h, r: (0, 0)),
        ],
        out_specs=pl.BlockSpec((1, rows, dk), lambda b, h, r: (b, r, h)),
        out_shape=jax.ShapeDtypeStruct((bsz, s, n_heads * dk), BF16),
        scratch_shapes=[
            pltpu.VMEM((dk, dk), F32),
            pltpu.VMEM((rows, dk), F32),
            pltpu.VMEM((rows, dk), F32),
            pltpu.VMEM((rows, dk), F32),
            pltpu.VMEM((rows, dk), BF16),
            pltpu.VMEM((rows, dk), F32),
            pltpu.VMEM((rows, HGRN_SUB * dk), BF16),
        ],
        compiler_params=_cparams(("parallel", "parallel", "arbitrary")),
        name="hgrn",
    )(lb_logits, hg, hg, hg, hg, norm_g, sel)


def _dup_heads(w, n_heads):
    d = w.shape[0]
    w = w.reshape(d, n_heads, 1, HEAD_DIM)
    return jnp.broadcast_to(w, (d, n_heads, 2, HEAD_DIM)).reshape(d, n_heads * 2 * HEAD_DIM)


def _pick(n, prefs):
    for c in prefs:
        if n % c == 0:
            return c
    raise ValueError(f"no tile for {n}")


def kernel(x, p, ffn_norm, ffn_w_in, ffn_w_out, mix_norm, even_w_in, even_w_out, swa_sinks, conv_w, conv_b,
           conv_ln_g, conv_ln_b, odd_w_in, odd_w_out, hgrn_lb_logits, hgrn_norm, ple_norm, ple_w_gate,
           ple_w_proj, final_norm):
    bsz, s, d = x.shape
    depth = p.shape[0]
    m = bsz * s
    mix = d // 2
    d_ff = ffn_w_out.shape[2]
    swa_heads = mix // HEAD_DIM
    swa_kv = swa_heads // 4
    sb_heads = mix // HEAD_DIM
    hgrn_heads = mix // HGRN_DK
    nq = swa_heads * HEAD_DIM
    nkv = swa_kv * HEAD_DIM

    bm = _pick(m, (1024, 512, 256))
    bm_ffn = _pick(m, (512, 256))
    bn_ff = _pick(d_ff, (512, 256))
    bn_d = _pick(d, (1024, 512))
    bn_mix = _pick(mix, (1024, 512, 256))

    row = lambda v: v.reshape(1, -1).astype(F32)
    xf = x.reshape(m, d)
    for i in range(depth):
        j = i // 2
        xf = _ffn(xf, row(ffn_norm[i, 0]), ffn_w_in[i, 0].astype(BF16), ffn_w_out[i, 0].astype(BF16),
                  bm_ffn, bn_ff)
        if i % 2 == 0:
            w_in = even_w_in[j]
            w_qkv = jnp.concatenate(
                [w_in[:, :nq], _dup_heads(w_in[:, nq:nq + nkv], swa_kv),
                 _dup_heads(w_in[:, nq + nkv:nq + 2 * nkv], swa_kv)], axis=1).astype(BF16)
            qkv = _norm_proj(xf, row(mix_norm[i]), w_qkv, BF16, bm, _pick(w_qkv.shape[1], (1024, 512, 256)))
            a_out = _swa(qkv.reshape(bsz, s, -1), swa_sinks[j].astype(F32), swa_heads, swa_kv)
            c = _norm_glu(xf, row(mix_norm[i]), w_in[:, nq + 2 * nkv:].astype(BF16), bm, bn_mix)
            b_out = _conv(c.reshape(bsz, s, mix), conv_w[j].astype(F32), row(conv_b[j]), row(conv_ln_g[j]),
                          row(conv_ln_b[j]), _pick(s, (512, 256, 128)))
            xf = _out_proj(xf, a_out.reshape(m, mix), b_out.reshape(m, mix), even_w_out[j].astype(BF16),
                           bm, bn_d)
        else:
            w_in = odd_w_in[j]
            nsb = 3 * sb_heads * HEAD_DIM
            qkv = _norm_proj(xf, row(mix_norm[i]), w_in[:, :nsb].astype(BF16), BF16, bm,
                             _pick(nsb, (1024, 512, 256)))
            c_out = _sb_attn(qkv.reshape(bsz, s, nsb), sb_heads)
            hg = _norm_proj(xf, row(mix_norm[i]), w_in[:, nsb:].astype(BF16), F32, bm,
                            _pick(w_in.shape[1] - nsb, (1024, 512, 256)))
            d_out = _hgrn(hg.reshape(bsz, s, -1), hgrn_lb_logits.astype(F32), row(hgrn_norm[j]), j, hgrn_heads)
            xf = _out_proj(xf, c_out.reshape(m, mix), d_out.reshape(m, mix), odd_w_out[j].astype(BF16),
                           bm, bn_d)
        xf = _ffn(xf, row(ffn_norm[i, 1]), ffn_w_in[i, 1].astype(BF16), ffn_w_out[i, 1].astype(BF16),
                  bm_ffn, bn_ff)
        xf = _ple(xf, row(ple_norm[i]), ple_w_gate[i].astype(BF16), p[i].reshape(m, -1),
                  ple_w_proj[i].astype(BF16), bm, bn_d)
    return _final_norm(xf, row(final_norm), bm).reshape(bsz, s, d)
```

```python
import functools

import jax
import jax.numpy as jnp
from jax import lax
from jax.experimental import pallas as pl
from jax.experimental.pallas import tpu as pltpu

F32 = jnp.float32
BF16 = jnp.bfloat16

EPS = 1e-6
NEG_BIG = -1e30
LB_FLOOR = 1e-20
HEAD_DIM = 64
WINDOW = 128
CONV_WIDTH = 31
HGRN_DK = 128
LANES = 128
SUBLANES = 8
HGRN_SUB = 16
HGRN_ROWS = 256
SB_BLOCK = 128
SB_PAIRS = 4
LOG2E = 1.4426950408889634
SB_EXIT_LOG2 = -110.0 * LOG2E
VMEM_LIMIT = 52 * 1024 * 1024


def _cparams(sem):
    return pltpu.CompilerParams(dimension_semantics=sem, vmem_limit_bytes=VMEM_LIMIT)


def _rms(xf, g):
    ms = jnp.mean(xf * xf, axis=-1, keepdims=True)
    return xf * lax.rsqrt(ms + EPS) * g


def _sigmoid(a):
    return 1.0 / (1.0 + jnp.exp(-a))


def _ffn_kernel(x_ref, g_ref, wa_ref, wb_ref, wo_ref, o_ref, h_ref):
    @pl.when(pl.program_id(1) == 0)
    def _():
        xf = x_ref[...]
        h_ref[...] = _rms(xf, g_ref[...]).astype(BF16)
        o_ref[...] = xf

    h = h_ref[...]
    a = jnp.dot(h, wa_ref[...], preferred_element_type=F32)
    b = jnp.dot(h, wb_ref[...], preferred_element_type=F32)
    gg = (0.5 * (a * _sigmoid(a)) * b).astype(BF16)
    o_ref[...] += jnp.dot(gg, wo_ref[...], preferred_element_type=F32)


def _ffn(x, g, w_in, w_out, bm, bn):
    m, d = x.shape
    f = w_out.shape[0]
    nj = f // bn
    return pl.pallas_call(
        _ffn_kernel,
        grid=(m // bm, nj),
        in_specs=[
            pl.BlockSpec((bm, d), lambda i, j: (i, 0)),
            pl.BlockSpec((1, d), lambda i, j: (0, 0)),
            pl.BlockSpec((d, bn), lambda i, j: (0, j)),
            pl.BlockSpec((d, bn), lambda i, j: (0, j + nj)),
            pl.BlockSpec((bn, d), lambda i, j: (j, 0)),
        ],
        out_specs=pl.BlockSpec((bm, d), lambda i, j: (i, 0)),
        out_shape=jax.ShapeDtypeStruct((m, d), F32),
        scratch_shapes=[pltpu.VMEM((bm, d), BF16)],
        compiler_params=_cparams(("parallel", "arbitrary")),
        name="ffn",
    )(x, g, w_in, w_in, w_out)


def _norm_proj_kernel(x_ref, g_ref, w_ref, o_ref, h_ref):
    @pl.when(pl.program_id(1) == 0)
    def _():
        h_ref[...] = _rms(x_ref[...], g_ref[...]).astype(BF16)

    o_ref[...] = jnp.dot(h_ref[...], w_ref[...], preferred_element_type=F32).astype(o_ref.dtype)


def _norm_proj(x, g, w, out_dtype, bm, bn):
    m, d = x.shape
    n = w.shape[1]
    return pl.pallas_call(
        _norm_proj_kernel,
        grid=(m // bm, n // bn),
        in_specs=[
            pl.BlockSpec((bm, d), lambda i, j: (i, 0)),
            pl.BlockSpec((1, d), lambda i, j: (0, 0)),
            pl.BlockSpec((d, bn), lambda i, j: (0, j)),
        ],
        out_specs=pl.BlockSpec((bm, bn), lambda i, j: (i, j)),
        out_shape=jax.ShapeDtypeStruct((m, n), out_dtype),
        scratch_shapes=[pltpu.VMEM((bm, d), BF16)],
        compiler_params=_cparams(("parallel", "arbitrary")),
        name="norm_proj",
    )(x, g, w)


def _norm_glu_kernel(x_ref, g_ref, wa_ref, wb_ref, o_ref, h_ref):
    @pl.when(pl.program_id(1) == 0)
    def _():
        h_ref[...] = _rms(x_ref[...], g_ref[...]).astype(BF16)

    h = h_ref[...]
    a = jnp.dot(h, wa_ref[...], preferred_element_type=F32)
    b = jnp.dot(h, wb_ref[...], preferred_element_type=F32)
    o_ref[...] = a * _sigmoid(b)


def _norm_glu(x, g, w, bm, bn):
    m, d = x.shape
    c = w.shape[1] // 2
    nj = c // bn
    return pl.pallas_call(
        _norm_glu_kernel,
        grid=(m // bm, nj),
        in_specs=[
            pl.BlockSpec((bm, d), lambda i, j: (i, 0)),
            pl.BlockSpec((1, d), lambda i, j: (0, 0)),
            pl.BlockSpec((d, bn), lambda i, j: (0, j)),
            pl.BlockSpec((d, bn), lambda i, j: (0, j + nj)),
        ],
        out_specs=pl.BlockSpec((bm, bn), lambda i, j: (i, j)),
        out_shape=jax.ShapeDtypeStruct((m, c), F32),
        scratch_shapes=[pltpu.VMEM((bm, d), BF16)],
        compiler_params=_cparams(("parallel", "arbitrary")),
        name="norm_glu",
    )(x, g, w, w)


def _out_proj_kernel(x_ref, a_ref, b_ref, wa_ref, wb_ref, o_ref):
    acc = jnp.dot(a_ref[...], wa_ref[...], preferred_element_type=F32)
    acc += jnp.dot(b_ref[...], wb_ref[...], preferred_element_type=F32)
    o_ref[...] = x_ref[...] + acc


def _out_proj(x, a, b, w, bm, bn):
    m, d = x.shape
    ka = a.shape[1]
    assert ka == b.shape[1] and w.shape[0] == 2 * ka
    return pl.pallas_call(
        _out_proj_kernel,
        grid=(m // bm, d // bn),
        in_specs=[
            pl.BlockSpec((bm, bn), lambda i, j: (i, j)),
            pl.BlockSpec((bm, ka), lambda i, j: (i, 0)),
            pl.BlockSpec((bm, ka), lambda i, j: (i, 0)),
            pl.BlockSpec((ka, bn), lambda i, j: (0, j)),
            pl.BlockSpec((ka, bn), lambda i, j: (1, j)),
        ],
        out_specs=pl.BlockSpec((bm, bn), lambda i, j: (i, j)),
        out_shape=jax.ShapeDtypeStruct((m, d), F32),
        compiler_params=_cparams(("parallel", "arbitrary")),
        name="out_proj",
    )(x, a, b, w, w)


def _ple_kernel(x_ref, g_ref, wg_ref, p_ref, wp_ref, o_ref, h_ref, pb_ref, *, bn):
    j = pl.program_id(1)

    @pl.when(j == 0)
    def _():
        h_ref[...] = _rms(x_ref[...], g_ref[...]).astype(BF16)
        pb_ref[...] = p_ref[...].astype(BF16)

    gate = _sigmoid(jnp.dot(h_ref[...], wg_ref[...], preferred_element_type=F32))
    proj = jnp.dot(pb_ref[...], wp_ref[...], preferred_element_type=F32)
    xs = x_ref[:, pl.ds(pl.multiple_of(j * bn, bn), bn)]
    o_ref[...] = xs + gate * proj


def _ple(x, g, wg, p, wp, bm, bn):
    m, d = x.shape
    e = p.shape[1]
    return pl.pallas_call(
        functools.partial(_ple_kernel, bn=bn),
        grid=(m // bm, d // bn),
        in_specs=[
            pl.BlockSpec((bm, d), lambda i, j: (i, 0)),
            pl.BlockSpec((1, d), lambda i, j: (0, 0)),
            pl.BlockSpec((d, bn), lambda i, j: (0, j)),
            pl.BlockSpec((bm, e), lambda i, j: (i, 0)),
            pl.BlockSpec((e, bn), lambda i, j: (0, j)),
        ],
        out_specs=pl.BlockSpec((bm, bn), lambda i, j: (i, j)),
        out_shape=jax.ShapeDtypeStruct((m, d), F32),
        scratch_shapes=[pltpu.VMEM((bm, d), BF16), pltpu.VMEM((bm, e), BF16)],
        compiler_params=_cparams(("parallel", "arbitrary")),
        name="ple",
    )(x, g, wg, p, wp)


def _final_norm_kernel(x_ref, g_ref, o_ref):
    o_ref[...] = _rms(x_ref[...], g_ref[...])


def _final_norm(x, g, bm):
    m, d = x.shape
    return pl.pallas_call(
        _final_norm_kernel,
        grid=(m // bm,),
        in_specs=[pl.BlockSpec((bm, d), lambda i: (i, 0)), pl.BlockSpec((1, d), lambda i: (0, 0))],
        out_specs=pl.BlockSpec((bm, d), lambda i: (i, 0)),
        out_shape=jax.ShapeDtypeStruct((m, d), F32),
        compiler_params=_cparams(("parallel",)),
        name="final_norm",
    )(x, g)


def _swa_kernel(sink_ref, q_ref, kc_ref, kp_ref, vc_ref, vp_ref, o_ref, *, n_pairs, group):
    w = WINDOW
    n = pl.program_id(1)
    qi = lax.broadcasted_iota(jnp.int32, (2 * w, 2 * w), 0) & (w - 1)
    sj = lax.broadcasted_iota(jnp.int32, (2 * w, 2 * w), 1)
    rel = qi + w - sj
    first_key = jnp.where(n > 0, 0, w)
    valid2 = (rel >= 0) & (rel < w) & (sj >= first_key)
    top = lax.broadcasted_iota(jnp.int32, (2 * w, 1), 0) < w
    lo = lax.broadcasted_iota(jnp.int32, (1, LANES), 1) < HEAD_DIM
    scale = HEAD_DIM ** -0.5
    nt = (((1,), (1,)), ((), ()))
    logits, vds = [], []
    for p in range(n_pairs):
        kv = (2 * p) // group
        cs = slice(kv * LANES, (kv + 1) * LANES)
        qp = q_ref[0, :, p * LANES:(p + 1) * LANES]
        zero_q = jnp.zeros_like(qp)
        q2 = jnp.concatenate([jnp.where(lo, qp, zero_q), jnp.where(lo, zero_q, qp)], axis=0)
        kd = jnp.concatenate([kp_ref[0, :, cs], kc_ref[0, :, cs]], axis=0)
        vd = jnp.concatenate([vp_ref[0, :, cs], vc_ref[0, :, cs]], axis=0)
        zero_v = jnp.zeros_like(vd)
        vds.append(jnp.concatenate([jnp.where(lo, vd, zero_v), jnp.where(lo, zero_v, vd)], axis=0))
        logits.append(lax.dot_general(q2, kd, nt, preferred_element_type=F32))
    for p in range(n_pairs):
        lg = jnp.where(valid2, logits[p] * scale, NEG_BIG)
        sink = jnp.where(top, sink_ref[2 * p], sink_ref[2 * p + 1])
        mx = jnp.maximum(jnp.max(lg, axis=-1, keepdims=True), sink)
        e = jnp.exp(lg - mx)
        probs = (e / (jnp.sum(e, axis=-1, keepdims=True) + jnp.exp(sink - mx))).astype(BF16)
        acc = jnp.dot(jnp.concatenate([probs[:w], probs[w:]], axis=1), vds[p], preferred_element_type=F32)
        o_ref[0, :, p * LANES:(p + 1) * LANES] = acc.astype(o_ref.dtype)


def _swa(qkv, sinks, n_q_heads, n_kv_heads):
    bsz, s, _ = qkv.shape
    w = WINDOW
    nq = n_q_heads * HEAD_DIM
    nk = n_kv_heads * LANES
    assert nq % nk == 0
    kblk = nq // nk
    prev = lambda b, n: jnp.maximum(n - 1, 0)
    return pl.pallas_call(
        functools.partial(_swa_kernel, n_pairs=n_q_heads // 2, group=n_q_heads // n_kv_heads),
        grid=(bsz, s // w),
        in_specs=[
            pl.BlockSpec(memory_space=pltpu.SMEM),
            pl.BlockSpec((1, w, nq), lambda b, n: (b, n, 0)),
            pl.BlockSpec((1, w, nk), lambda b, n: (b, n, kblk)),
            pl.BlockSpec((1, w, nk), lambda b, n: (b, prev(b, n), kblk)),
            pl.BlockSpec((1, w, nk), lambda b, n: (b, n, kblk + 1)),
            pl.BlockSpec((1, w, nk), lambda b, n: (b, prev(b, n), kblk + 1)),
        ],
        out_specs=pl.BlockSpec((1, w, nq), lambda b, n: (b, n, 0)),
        out_shape=jax.ShapeDtypeStruct((bsz, s, nq), BF16),
        compiler_params=_cparams(("parallel", "arbitrary")),
        name="swa",
    )(sinks, qkv, qkv, qkv, qkv, qkv)


def _conv_kernel(cur_ref, halo_ref, w_ref, b_ref, lg_ref, lb_ref, o_ref, slab_ref, sh_ref, y_ref, *,
                 bs, halo, rows, lane_group):
    i = pl.program_id(1)
    ch = cur_ref.shape[2]
    hv = halo_ref[0]
    slab_ref[0:halo, :] = jnp.where(i > 0, hv, jnp.zeros_like(hv))
    slab_ref[halo:, :] = cur_ref[0]
    off = halo - (CONV_WIDTH - 1)
    for ph in range(SUBLANES):
        sh_ref[ph, 0:bs + halo - ph, :] = slab_ref[ph:bs + halo, :]

    def taps(r, carry):
        base = pl.multiple_of(r * rows, rows)
        for g in range(ch // lane_group):
            ls = slice(g * lane_group, (g + 1) * lane_group)
            accs = [jnp.zeros((SUBLANES, lane_group), F32) + b_ref[:, ls] for _ in range(rows // SUBLANES)]
            for t in range(CONV_WIDTH):
                ph, al = (off + t) % SUBLANES, (off + t) // SUBLANES * SUBLANES
                wt = w_ref[t, :, ls]
                for u in range(rows // SUBLANES):
                    accs[u] = accs[u] + sh_ref[ph, pl.ds(base + al + u * SUBLANES, SUBLANES), ls] * wt
            for u in range(rows // SUBLANES):
                y_ref[pl.ds(base + u * SUBLANES, SUBLANES), ls] = accs[u]
        return carry

    lax.fori_loop(0, bs // rows, taps, 0)

    acc = y_ref[...]
    mu = jnp.mean(acc, axis=-1, keepdims=True)
    cen = acc - mu
    var = jnp.mean(cen * cen, axis=-1, keepdims=True)
    y = cen * lax.rsqrt(var + EPS) * lg_ref[...] + lb_ref[...]
    o_ref[0] = (y * _sigmoid(y)).astype(o_ref.dtype)


def _conv(c, conv_w, conv_b, ln_g, ln_b, bs):
    bsz, s, ch = c.shape
    halo = 32
    w_rep = jnp.broadcast_to(conv_w[:, None, :], (CONV_WIDTH, SUBLANES, ch))
    return pl.pallas_call(
        functools.partial(_conv_kernel, bs=bs, halo=halo, rows=32, lane_group=256),
        grid=(bsz, s // bs),
        in_specs=[
            pl.BlockSpec((1, bs, ch), lambda b, i: (b, i, 0)),
            pl.BlockSpec((1, halo, ch), lambda b, i: (b, jnp.maximum(i * (bs // halo) - 1, 0), 0)),
            pl.BlockSpec((CONV_WIDTH, SUBLANES, ch), lambda b, i: (0, 0, 0)),
            pl.BlockSpec((1, ch), lambda b, i: (0, 0)),
            pl.BlockSpec((1, ch), lambda b, i: (0, 0)),
            pl.BlockSpec((1, ch), lambda b, i: (0, 0)),
        ],
        out_specs=pl.BlockSpec((1, bs, ch), lambda b, i: (b, i, 0)),
        out_shape=jax.ShapeDtypeStruct((bsz, s, ch), BF16),
        scratch_shapes=[pltpu.VMEM((bs + halo, ch), F32), pltpu.VMEM((SUBLANES, bs + halo, ch), F32),
                        pltpu.VMEM((bs, ch), F32)],
        compiler_params=_cparams(("parallel", "arbitrary")),
        name="conv",
    )(c, c, w_rep, conv_b, ln_g, ln_b)


def _sb_kernel(q_ref, k_ref, v_ref, o_ref, acc_ref, carry_ref, *, n_pairs):
    blk = SB_BLOCK
    qb = pl.program_id(2)
    lo = lax.broadcasted_iota(jnp.int32, (1, LANES), 1) < HEAD_DIM
    row = lax.broadcasted_iota(jnp.int32, (2 * blk, blk), 0)
    col = lax.broadcasted_iota(jnp.int32, (2 * blk, blk), 1)
    causal = col < (row & (blk - 1))
    r2 = lax.broadcasted_iota(jnp.int32, (blk, 2 * blk), 0)
    c2 = lax.broadcasted_iota(jnp.int32, (blk, 2 * blk), 1)
    tri_ext = jnp.where((r2 > c2) | (c2 >= blk), 1.0, 0.0).astype(BF16)
    zscale = (HEAD_DIM ** -0.5) * LOG2E

    def block(kb, diag):
        start = pl.multiple_of(kb * blk, blk)
        cols = [slice(p * LANES, (p + 1) * LANES) for p in range(n_pairs)]
        zs = []
        for cs in cols:
            qp = q_ref[0, :, cs]
            zero_q = jnp.zeros_like(qp)
            q2 = jnp.concatenate([jnp.where(lo, qp, zero_q), jnp.where(lo, zero_q, qp)], axis=0)
            ks = k_ref[0, pl.ds(start, blk), cs]
            zs.append(lax.dot_general(q2, ks, (((1,), (1,)), ((), ())), preferred_element_type=F32))
        log_betas, sums = [], []
        for z in zs:
            z = z * zscale
            sp = jnp.log2(1.0 + jnp.exp2(-jnp.abs(z)))
            log_betas.append(jnp.minimum(z, 0.0) - sp)
            log_1m = -(jnp.maximum(z, 0.0) + sp)
            if diag:
                log_1m = jnp.where(causal, log_1m, 0.0)
            hi = log_1m.astype(BF16)
            rest = (log_1m - hi.astype(F32)).astype(BF16)
            sm = jnp.dot(jnp.concatenate([hi, rest], axis=0), tri_ext, preferred_element_type=F32)
            sums.append(sm[:2 * blk] + sm[2 * blk:])
        for p, cs in enumerate(cols):
            carry = carry_ref[p]
            a = jnp.exp2(log_betas[p] + sums[p][:, :blk] + carry)
            if diag:
                a = jnp.where(causal, a, 0.0)
            carry_ref[p] = carry + sums[p][:, blk:]
            ab = a.astype(BF16)
            vs = v_ref[0, pl.ds(start, blk), cs]
            zero_v = jnp.zeros_like(vs)
            v2 = jnp.concatenate([jnp.where(lo, vs, zero_v), jnp.where(lo, zero_v, vs)], axis=0)
            acc_ref[:, cs] += jnp.dot(jnp.concatenate([ab[:blk], ab[blk:]], axis=1), v2,
                                      preferred_element_type=F32)

    acc_ref[...] = jnp.zeros_like(acc_ref)
    carry_ref[...] = jnp.zeros_like(carry_ref)
    block(qb, True)

    def cond(kb):
        return jnp.logical_and(kb >= 0, jnp.max(carry_ref[...]) > SB_EXIT_LOG2)

    def body(kb):
        block(kb, False)
        return kb - 1

    lax.while_loop(cond, body, qb - 1)
    o_ref[0] = acc_ref[...].astype(o_ref.dtype)


def _sb_attn(qkv, n_heads):
    bsz, s, _ = qkv.shape
    npair = n_heads * HEAD_DIM // LANES
    pp = min(SB_PAIRS, npair)
    ng = npair // pp
    blk = SB_BLOCK
    wide = pp * LANES
    return pl.pallas_call(
        functools.partial(_sb_kernel, n_pairs=pp),
        grid=(bsz, ng, s // blk),
        in_specs=[
            pl.BlockSpec((1, blk, wide), lambda b, g, i: (b, i, g)),
            pl.BlockSpec((1, s, wide), lambda b, g, i: (b, 0, ng + g)),
            pl.BlockSpec((1, s, wide), lambda b, g, i: (b, 0, 2 * ng + g)),
        ],
        out_specs=pl.BlockSpec((1, blk, wide), lambda b, g, i: (b, i, g)),
        out_shape=jax.ShapeDtypeStruct((bsz, s, npair * LANES), BF16),
        scratch_shapes=[pltpu.VMEM((blk, wide), F32), pltpu.VMEM((pp, 2 * blk, blk), F32)],
        compiler_params=_cparams(("parallel", "parallel", "arbitrary")),
        name="sb_attn",
    )(qkv, qkv, qkv)


def _hgrn_kernel(lbl_ref, qd_ref, fz_ref, iv_ref, gt_ref, ng_ref, sel_ref, o_ref,
                 state_ref, q_s, k_s, b_s, pcat_s, *, layer, rows):
    sub = HGRN_SUB
    nsub = rows // sub

    @pl.when(pl.program_id(2) == 0)
    def _():
        state_ref[...] = jnp.zeros_like(state_ref)

    logits = lbl_ref[...]
    ex = jnp.exp(logits - jnp.max(logits, axis=0, keepdims=True))
    sm = ex / jnp.sum(ex, axis=0, keepdims=True)
    cum = sm[0:1, :]
    for r in range(1, layer + 1):
        cum = cum + sm[r:r + 1, :]
    lb = cum - sm[0:1, :]

    fz = fz_ref[0]
    e = jnp.exp(-jnp.abs(fz))
    log_sig = jnp.minimum(fz, 0.0) - jnp.log(1.0 + e)
    la = jnp.log(jnp.maximum(lb, LB_FLOOR))
    lc = jnp.log1p(-lb) + log_sig
    logf2 = (jnp.maximum(la, lc) + jnp.log(1.0 + jnp.exp(-jnp.abs(la - lc)))) * LOG2E
    kk = (1.0 - lb) * (jnp.where(fz >= 0.0, e, 1.0) / (1.0 + e))
    k_s[...] = kk
    qd = qd_ref[0]
    qq = qd * _sigmoid(qd)
    q_s[...] = qq
    vv = iv_ref[0].astype(BF16)

    ri = lax.broadcasted_iota(jnp.int32, (rows, rows), 0)
    ci = lax.broadcasted_iota(jnp.int32, (rows, rows), 1)
    same = (ri // sub) == (ci // sub)
    lower = same & (ci <= ri)
    cum_mat = jnp.where(lower, 1.0, 0.0).astype(BF16)
    p0 = logf2.astype(BF16)
    r0 = logf2 - p0.astype(F32)
    p1 = r0.astype(BF16)
    p2 = (r0 - p1.astype(F32)).astype(BF16)
    b2 = (jnp.dot(cum_mat, p0, preferred_element_type=F32) + jnp.dot(cum_mat, p1, preferred_element_type=F32)
          + jnp.dot(cum_mat, p2, preferred_element_type=F32))
    b_s[...] = b2

    def diag_body(i, carry):
        base = pl.multiple_of(i * sub, sub)
        bi = b_s[pl.ds(base, sub), :]
        qi = q_s[pl.ds(base, sub), :]
        for s in range(sub):
            brow = b_s[pl.ds(base + s, 1), :]
            krow = k_s[pl.ds(base + s, 1), :]
            dec = jnp.exp2(jnp.minimum(bi - brow, 0.0))
            pcat_s[pl.ds(base, sub), s * LANES:(s + 1) * LANES] = ((qi * krow) * dec).astype(BF16)
        return carry

    lax.fori_loop(0, nsub, diag_body, 0)
    s_rep = jnp.dot(pcat_s[...], sel_ref[...], preferred_element_type=F32)
    s_bd = jnp.where(lower, s_rep, 0.0).astype(BF16)
    o_intra = jnp.dot(s_bd, vv, preferred_element_type=F32)

    qe = (qq * jnp.exp2(b2)).astype(BF16)
    chunks = [slice(c * sub, (c + 1) * sub) for c in range(nsub)]
    lasts = [b2[(c + 1) * sub - 1:(c + 1) * sub, :] for c in range(nsub)]
    kvs = []
    for rs, bl in zip(chunks, lasts):
        ke = (kk[rs] * jnp.exp2(bl - b2[rs])).astype(BF16)
        kvs.append(lax.dot_general(vv[rs], ke, (((0,), (0,)), ((), ())), preferred_element_type=F32))
    st = state_ref[...]
    states = []
    for bl, kv in zip(lasts, kvs):
        states.append(st.astype(BF16))
        st = st * jnp.exp2(bl) + kv
    state_ref[...] = st
    outs = [lax.dot_general(qe[rs], sb, (((1,), (1,)), ((), ())), preferred_element_type=F32)
            for rs, sb in zip(chunks, states)]
    o = o_intra + jnp.concatenate(outs, axis=0)
    o = o * lax.rsqrt(jnp.mean(o * o, axis=-1, keepdims=True) + EPS) * ng_ref[...]
    gt = gt_ref[0]
    o_ref[0] = (o * (gt * _sigmoid(gt))).astype(o_ref.dtype)


def _hgrn(hg, lb_logits, norm_g, layer, n_heads):
    bsz, s, _ = hg.shape
    rows = HGRN_ROWS
    dk = HGRN_DK
    nl = lb_logits.shape[0]
    kk = lax.broadcasted_iota(jnp.int32, (HGRN_SUB * dk, rows), 0) // dk
    cc = lax.broadcasted_iota(jnp.int32, (HGRN_SUB * dk, rows), 1) % HGRN_SUB
    sel = (kk == cc).astype(BF16)
    blk = lambda off: pl.BlockSpec((1, rows, dk), lambda b, h, r: (b, r, off * n_heads + h))
    return pl.pallas_call(
        functools.partial(_hgrn_kernel, layer=layer, rows=rows),
        grid=(bsz, n_heads, s // rows),
        in_specs=[
            pl.BlockSpec((nl, dk), lambda b, h, r: (0, h)),
            blk(0), blk(1), blk(2), blk(3),
            pl.BlockSpec((1, dk), lambda b, h, r: (0, h)),
            pl.BlockSpec((HGRN_SUB * dk, rows), lambda b, h, r: (0, 0)),
        ],
        out_specs=pl.BlockSpec((1, rows, dk), lambda b, h, r: (b, r, h)),
        out_shape=jax.ShapeDtypeStruct((bsz, s, n_heads * dk), BF16),
        scratch_shapes=[
            pltpu.VMEM((dk, dk), F32),
            pltpu.VMEM((rows, dk), F32),
            pltpu.VMEM((rows, dk), F32),
            pltpu.VMEM((rows, dk), F32),
            pltpu.VMEM((rows, HGRN_SUB * dk), BF16),
        ],
        compiler_params=_cparams(("parallel", "parallel", "arbitrary")),
        name="hgrn",
    )(lb_logits, hg, hg, hg, hg, norm_g, sel)


def _dup_heads(w, n_heads):
    d = w.shape[0]
    w = w.reshape(d, n_heads, 1, HEAD_DIM)
    return jnp.broadcast_to(w, (d, n_heads, 2, HEAD_DIM)).reshape(d, n_heads * 2 * HEAD_DIM)


def _pick(n, prefs):
    for c in prefs:
        if n % c == 0:
            return c
    raise ValueError(f"no tile for {n}")


def kernel(x, p, ffn_norm, ffn_w_in, ffn_w_out, mix_norm, even_w_in, even_w_out, swa_sinks, conv_w, conv_b,
           conv_ln_g, conv_ln_b, odd_w_in, odd_w_out, hgrn_lb_logits, hgrn_norm, ple_norm, ple_w_gate,
           ple_w_proj, final_norm):
    bsz, s, d = x.shape
    depth = p.shape[0]
    m = bsz * s
    mix = d // 2
    d_ff = ffn_w_out.shape[2]
    swa_heads = mix // HEAD_DIM
    swa_kv = swa_heads // 4
    sb_heads = mix // HEAD_DIM
    hgrn_heads = mix // HGRN_DK
    nq = swa_heads * HEAD_DIM
    nkv = swa_kv * HEAD_DIM

    bm = _pick(m, (1024, 512, 256))
    bm_ffn = _pick(m, (512, 256))
    bn_ff = _pick(d_ff, (512, 256))
    bn_d = _pick(d, (1024, 512))
    bn_mix = _pick(mix, (1024, 512, 256))

    row = lambda v: v.reshape(1, -1).astype(F32)
    xf = x.reshape(m, d)
    for i in range(depth):
        j = i // 2
        xf = _ffn(xf, row(ffn_norm[i, 0]), ffn_w_in[i, 0].astype(BF16), ffn_w_out[i, 0].astype(BF16),
                  bm_ffn, bn_ff)
        if i % 2 == 0:
            w_in = even_w_in[j]
            w_qkv = jnp.concatenate(
                [w_in[:, :nq], _dup_heads(w_in[:, nq:nq + nkv], swa_kv),
                 _dup_heads(w_in[:, nq + nkv:nq + 2 * nkv], swa_kv)], axis=1).astype(BF16)
            qkv = _norm_proj(xf, row(mix_norm[i]), w_qkv, BF16, bm, _pick(w_qkv.shape[1], (1024, 512, 256)))
            a_out = _swa(qkv.reshape(bsz, s, -1), swa_sinks[j].astype(F32), swa_heads, swa_kv)
            c = _norm_glu(xf, row(mix_norm[i]), w_in[:, nq + 2 * nkv:].astype(BF16), bm, bn_mix)
            b_out = _conv(c.reshape(bsz, s, mix), conv_w[j].astype(F32), row(conv_b[j]), row(conv_ln_g[j]),
                          row(conv_ln_b[j]), _pick(s, (256, 128)))
            xf = _out_proj(xf, a_out.reshape(m, mix), b_out.reshape(m, mix), even_w_out[j].astype(BF16),
                           bm, bn_d)
        else:
            w_in = odd_w_in[j]
            nsb = 3 * sb_heads * HEAD_DIM
            qkv = _norm_proj(xf, row(mix_norm[i]), w_in[:, :nsb].astype(BF16), BF16, bm,
                             _pick(nsb, (1024, 512, 256)))
            c_out = _sb_attn(qkv.reshape(bsz, s, nsb), sb_heads)
            hg = _norm_proj(xf, row(mix_norm[i]), w_in[:, nsb:].astype(BF16), F32, bm,
                            _pick(w_in.shape[1] - nsb, (1024, 512, 256)))
            d_out = _hgrn(hg.reshape(bsz, s, -1), hgrn_lb_logits.astype(F32), row(hgrn_norm[j]), j, hgrn_heads)
            xf = _out_proj(xf, c_out.reshape(m, mix), d_out.reshape(m, mix), odd_w_out[j].astype(BF16),
                           bm, bn_d)
        xf = _ffn(xf, row(ffn_norm[i, 1]), ffn_w_in[i, 1].astype(BF16), ffn_w_out[i, 1].astype(BF16),
                  bm_ffn, bn_ff)
        xf = _ple(xf, row(ple_norm[i]), ple_w_gate[i].astype(BF16), p[i].reshape(m, -1),
                  ple_w_proj[i].astype(BF16), bm, bn_d)
    return _final_norm(xf, row(final_norm), bm).reshape(bsz, s, d)
```

```python
import functools

import jax
import jax.numpy as jnp
from jax import lax
from jax.experimental import pallas as pl
from jax.experimental.pallas import tpu as pltpu

F32 = jnp.float32
BF16 = jnp.bfloat16

EPS = 1e-6
NEG_BIG = -1e30
LB_FLOOR = 1e-20
HEAD_DIM = 64
WINDOW = 128
CONV_WIDTH = 31
HGRN_DK = 128
LANES = 128
SUBLANES = 8
HGRN_SUB = 16
HGRN_ROWS = 256
HGRN_HEADS_PER_STEP = 2
ROW_SUB = 256
SB_BLOCK = 128
SB_PAIRS = 8
LOG2E = 1.4426950408889634
SB_EXIT_LOG2 = -110.0 * LOG2E
VMEM_LIMIT = 58 * 1024 * 1024


def _cparams(sem):
    return pltpu.CompilerParams(dimension_semantics=sem, vmem_limit_bytes=VMEM_LIMIT)


def _rms(xf, g):
    ms = jnp.mean(xf * xf, axis=-1, keepdims=True)
    return xf * lax.rsqrt(ms + EPS) * g


def _sigmoid(a):
    return 1.0 / (1.0 + jnp.exp(-a))


def _row_tiles(n):
    return [slice(r, r + ROW_SUB) for r in range(0, n, ROW_SUB)]


def _ffn_kernel(x_ref, g_ref, wa_ref, wb_ref, wo_ref, o_ref, h_ref):
    def tile(h, base):
        a = jnp.dot(h, wa_ref[...], preferred_element_type=F32)
        b = jnp.dot(h, wb_ref[...], preferred_element_type=F32)
        gg = (0.5 * (a * _sigmoid(a)) * b).astype(BF16)
        return base + jnp.dot(gg, wo_ref[...], preferred_element_type=F32)

    def step(first):
        if first:
            for rs in _row_tiles(x_ref.shape[0]):
                xs = x_ref[rs, :]
                h = _rms(xs, g_ref[...]).astype(BF16)
                h_ref[rs, :] = h
                o_ref[rs, :] = tile(h, xs)
        else:
            o_ref[...] = tile(h_ref[...], o_ref[...])

    j = pl.program_id(1)
    pl.when(j == 0)(lambda: step(True))
    pl.when(j > 0)(lambda: step(False))


def _ffn(x, g, w_in, w_out, bm, bn):
    m, d = x.shape
    f = w_out.shape[0]
    nj = f // bn
    return pl.pallas_call(
        _ffn_kernel,
        grid=(m // bm, nj),
        in_specs=[
            pl.BlockSpec((bm, d), lambda i, j: (i, 0)),
            pl.BlockSpec((1, d), lambda i, j: (0, 0)),
            pl.BlockSpec((d, bn), lambda i, j: (0, j)),
            pl.BlockSpec((d, bn), lambda i, j: (0, j + nj)),
            pl.BlockSpec((bn, d), lambda i, j: (j, 0)),
        ],
        out_specs=pl.BlockSpec((bm, d), lambda i, j: (i, 0)),
        out_shape=jax.ShapeDtypeStruct((m, d), F32),
        scratch_shapes=[pltpu.VMEM((bm, d), BF16)],
        compiler_params=_cparams(("parallel", "arbitrary")),
        name="ffn",
    )(x, g, w_in, w_in, w_out)


def _resident(shape):
    return pl.BlockSpec(shape, lambda i: (0,) * len(shape), pipeline_mode=pl.Buffered(1))


def _norm_proj_kernel(x_ref, g_ref, w_ref, o_ref):
    for rs in _row_tiles(x_ref.shape[0]):
        h = _rms(x_ref[rs, :], g_ref[...]).astype(BF16)
        o_ref[rs, :] = jnp.dot(h, w_ref[...], preferred_element_type=F32).astype(o_ref.dtype)


def _norm_proj(x, g, w, out_dtype, bm):
    m, d = x.shape
    n = w.shape[1]
    return pl.pallas_call(
        _norm_proj_kernel,
        grid=(m // bm,),
        in_specs=[pl.BlockSpec((bm, d), lambda i: (i, 0)), _resident((1, d)), _resident((d, n))],
        out_specs=pl.BlockSpec((bm, n), lambda i: (i, 0)),
        out_shape=jax.ShapeDtypeStruct((m, n), out_dtype),
        compiler_params=_cparams(("parallel",)),
        name="norm_proj",
    )(x, g, w)


def _norm_glu_kernel(x_ref, g_ref, w_ref, o_ref):
    c = o_ref.shape[1]
    for rs in _row_tiles(x_ref.shape[0]):
        h = _rms(x_ref[rs, :], g_ref[...]).astype(BF16)
        a = jnp.dot(h, w_ref[:, :c], preferred_element_type=F32)
        b = jnp.dot(h, w_ref[:, c:], preferred_element_type=F32)
        o_ref[rs, :] = a * _sigmoid(b)


def _norm_glu(x, g, w, bm):
    m, d = x.shape
    c = w.shape[1] // 2
    return pl.pallas_call(
        _norm_glu_kernel,
        grid=(m // bm,),
        in_specs=[pl.BlockSpec((bm, d), lambda i: (i, 0)), _resident((1, d)), _resident((d, 2 * c))],
        out_specs=pl.BlockSpec((bm, c), lambda i: (i, 0)),
        out_shape=jax.ShapeDtypeStruct((m, c), F32),
        compiler_params=_cparams(("parallel",)),
        name="norm_glu",
    )(x, g, w)


def _out_proj_kernel(x_ref, a_ref, b_ref, w_ref, o_ref):
    ka = a_ref.shape[1]
    for rs in _row_tiles(x_ref.shape[0]):
        acc = jnp.dot(a_ref[rs, :], w_ref[:ka, :], preferred_element_type=F32)
        acc += jnp.dot(b_ref[rs, :], w_ref[ka:, :], preferred_element_type=F32)
        o_ref[rs, :] = x_ref[rs, :] + acc


def _out_proj(x, a, b, w, bm):
    m, d = x.shape
    ka = a.shape[1]
    assert ka == b.shape[1] and w.shape[0] == 2 * ka
    return pl.pallas_call(
        _out_proj_kernel,
        grid=(m // bm,),
        in_specs=[
            pl.BlockSpec((bm, d), lambda i: (i, 0)),
            pl.BlockSpec((bm, ka), lambda i: (i, 0)),
            pl.BlockSpec((bm, ka), lambda i: (i, 0)),
            _resident((2 * ka, d)),
        ],
        out_specs=pl.BlockSpec((bm, d), lambda i: (i, 0)),
        out_shape=jax.ShapeDtypeStruct((m, d), F32),
        compiler_params=_cparams(("parallel",)),
        name="out_proj",
    )(x, a, b, w)


def _ple_kernel(x_ref, g_ref, wg_ref, p_ref, wp_ref, gf_ref, o_ref, *, final):
    for rs in _row_tiles(x_ref.shape[0]):
        xs = x_ref[rs, :]
        h = _rms(xs, g_ref[...]).astype(BF16)
        gate = _sigmoid(jnp.dot(h, wg_ref[...], preferred_element_type=F32))
        proj = jnp.dot(p_ref[rs, :].astype(BF16), wp_ref[...], preferred_element_type=F32)
        y = xs + gate * proj
        o_ref[rs, :] = _rms(y, gf_ref[...]) if final else y


def _ple(x, g, wg, p, wp, gf, final, bm):
    m, d = x.shape
    e = p.shape[1]
    return pl.pallas_call(
        functools.partial(_ple_kernel, final=final),
        grid=(m // bm,),
        in_specs=[
            pl.BlockSpec((bm, d), lambda i: (i, 0)),
            _resident((1, d)),
            _resident((d, d)),
            pl.BlockSpec((bm, e), lambda i: (i, 0)),
            _resident((e, d)),
            _resident((1, d)),
        ],
        out_specs=pl.BlockSpec((bm, d), lambda i: (i, 0)),
        out_shape=jax.ShapeDtypeStruct((m, d), F32),
        compiler_params=_cparams(("parallel",)),
        name="ple",
    )(x, g, wg, p, wp, gf)


def _swa_kernel(sink_ref, q_ref, kc_ref, kp_ref, vc_ref, vp_ref, o_ref, *, n_pairs, group):
    w = WINDOW
    n = pl.program_id(1)
    qi = lax.broadcasted_iota(jnp.int32, (2 * w, 2 * w), 0) & (w - 1)
    sj = lax.broadcasted_iota(jnp.int32, (2 * w, 2 * w), 1)
    rel = qi + w - sj
    first_key = jnp.where(n > 0, 0, w)
    valid2 = (rel >= 0) & (rel < w) & (sj >= first_key)
    top = lax.broadcasted_iota(jnp.int32, (2 * w, 1), 0) < w
    lo = lax.broadcasted_iota(jnp.int32, (1, LANES), 1) < HEAD_DIM
    scale = HEAD_DIM ** -0.5
    nt = (((1,), (1,)), ((), ()))
    logits, vds = [], []
    for p in range(n_pairs):
        kv = (2 * p) // group
        cs = slice(kv * LANES, (kv + 1) * LANES)
        qp = q_ref[0, :, p * LANES:(p + 1) * LANES]
        zero_q = jnp.zeros_like(qp)
        q2 = jnp.concatenate([jnp.where(lo, qp, zero_q), jnp.where(lo, zero_q, qp)], axis=0)
        kd = jnp.concatenate([kp_ref[0, :, cs], kc_ref[0, :, cs]], axis=0)
        vd = jnp.concatenate([vp_ref[0, :, cs], vc_ref[0, :, cs]], axis=0)
        zero_v = jnp.zeros_like(vd)
        vds.append(jnp.concatenate([jnp.where(lo, vd, zero_v), jnp.where(lo, zero_v, vd)], axis=0))
        logits.append(lax.dot_general(q2, kd, nt, preferred_element_type=F32))
    for p in range(n_pairs):
        lg = jnp.where(valid2, logits[p] * scale, NEG_BIG)
        sink = jnp.where(top, sink_ref[2 * p], sink_ref[2 * p + 1])
        mx = jnp.maximum(jnp.max(lg, axis=-1, keepdims=True), sink)
        e = jnp.exp(lg - mx)
        probs = (e / (jnp.sum(e, axis=-1, keepdims=True) + jnp.exp(sink - mx))).astype(BF16)
        acc = jnp.dot(jnp.concatenate([probs[:w], probs[w:]], axis=1), vds[p], preferred_element_type=F32)
        o_ref[0, :, p * LANES:(p + 1) * LANES] = acc.astype(o_ref.dtype)


def _swa(qkv, sinks, n_q_heads, n_kv_heads):
    bsz, s, _ = qkv.shape
    w = WINDOW
    nq = n_q_heads * HEAD_DIM
    nk = n_kv_heads * LANES
    assert nq % nk == 0
    kblk = nq // nk
    prev = lambda b, n: jnp.maximum(n - 1, 0)
    return pl.pallas_call(
        functools.partial(_swa_kernel, n_pairs=n_q_heads // 2, group=n_q_heads // n_kv_heads),
        grid=(bsz, s // w),
        in_specs=[
            pl.BlockSpec(memory_space=pltpu.SMEM),
            pl.BlockSpec((1, w, nq), lambda b, n: (b, n, 0)),
            pl.BlockSpec((1, w, nk), lambda b, n: (b, n, kblk)),
            pl.BlockSpec((1, w, nk), lambda b, n: (b, prev(b, n), kblk)),
            pl.BlockSpec((1, w, nk), lambda b, n: (b, n, kblk + 1)),
            pl.BlockSpec((1, w, nk), lambda b, n: (b, prev(b, n), kblk + 1)),
        ],
        out_specs=pl.BlockSpec((1, w, nq), lambda b, n: (b, n, 0)),
        out_shape=jax.ShapeDtypeStruct((bsz, s, nq), BF16),
        compiler_params=_cparams(("parallel", "arbitrary")),
        name="swa",
    )(sinks, qkv, qkv, qkv, qkv, qkv)


def _conv_kernel(cur_ref, halo_ref, w_ref, b_ref, lg_ref, lb_ref, o_ref, slab_ref, sh_ref, y_ref, *,
                 bs, halo, rows, lane_group):
    i = pl.program_id(1)
    ch = cur_ref.shape[2]
    hv = halo_ref[0]
    slab_ref[0:halo, :] = jnp.where(i > 0, hv, jnp.zeros_like(hv))
    slab_ref[halo:, :] = cur_ref[0]
    off = halo - (CONV_WIDTH - 1)
    for ph in range(SUBLANES):
        sh_ref[ph, 0:bs + halo - ph, :] = slab_ref[ph:bs + halo, :]

    def taps(r, carry):
        base = pl.multiple_of(r * rows, rows)
        for g in range(ch // lane_group):
            ls = slice(g * lane_group, (g + 1) * lane_group)
            accs = [jnp.zeros((SUBLANES, lane_group), F32) + b_ref[:, ls] for _ in range(rows // SUBLANES)]
            for t in range(CONV_WIDTH):
                ph, al = (off + t) % SUBLANES, (off + t) // SUBLANES * SUBLANES
                wt = w_ref[t, :, ls]
                for u in range(rows // SUBLANES):
                    accs[u] = accs[u] + sh_ref[ph, pl.ds(base + al + u * SUBLANES, SUBLANES), ls] * wt
            for u in range(rows // SUBLANES):
                y_ref[pl.ds(base + u * SUBLANES, SUBLANES), ls] = accs[u]
        return carry

    lax.fori_loop(0, bs // rows, taps, 0)

    acc = y_ref[...]
    mu = jnp.mean(acc, axis=-1, keepdims=True)
    cen = acc - mu
    var = jnp.mean(cen * cen, axis=-1, keepdims=True)
    y = cen * lax.rsqrt(var + EPS) * lg_ref[...] + lb_ref[...]
    o_ref[0] = (y * _sigmoid(y)).astype(o_ref.dtype)


def _conv(c, conv_w, conv_b, ln_g, ln_b, bs):
    bsz, s, ch = c.shape
    halo = 32
    w_rep = jnp.broadcast_to(conv_w[:, None, :], (CONV_WIDTH, SUBLANES, ch))
    return pl.pallas_call(
        functools.partial(_conv_kernel, bs=bs, halo=halo, rows=32, lane_group=256),
        grid=(bsz, s // bs),
        in_specs=[
            pl.BlockSpec((1, bs, ch), lambda b, i: (b, i, 0)),
            pl.BlockSpec((1, halo, ch), lambda b, i: (b, jnp.maximum(i * (bs // halo) - 1, 0), 0)),
            pl.BlockSpec((CONV_WIDTH, SUBLANES, ch), lambda b, i: (0, 0, 0)),
            pl.BlockSpec((1, ch), lambda b, i: (0, 0)),
            pl.BlockSpec((1, ch), lambda b, i: (0, 0)),
            pl.BlockSpec((1, ch), lambda b, i: (0, 0)),
        ],
        out_specs=pl.BlockSpec((1, bs, ch), lambda b, i: (b, i, 0)),
        out_shape=jax.ShapeDtypeStruct((bsz, s, ch), BF16),
        scratch_shapes=[pltpu.VMEM((bs + halo, ch), F32), pltpu.VMEM((SUBLANES, bs + halo, ch), F32),
                        pltpu.VMEM((bs, ch), F32)],
        compiler_params=_cparams(("parallel", "arbitrary")),
        name="conv",
    )(c, c, w_rep, conv_b, ln_g, ln_b)


def _sb_kernel(q_ref, k_ref, v_ref, o_ref, acc_ref, carry_ref, *, n_pairs):
    blk = SB_BLOCK
    qb = pl.program_id(2)
    lo = lax.broadcasted_iota(jnp.int32, (1, LANES), 1) < HEAD_DIM
    row = lax.broadcasted_iota(jnp.int32, (2 * blk, blk), 0)
    col = lax.broadcasted_iota(jnp.int32, (2 * blk, blk), 1)
    causal = col < (row & (blk - 1))
    causal_all = col >= 0
    r2 = lax.broadcasted_iota(jnp.int32, (blk, 2 * blk), 0)
    c2 = lax.broadcasted_iota(jnp.int32, (blk, 2 * blk), 1)
    tri_ext = jnp.where((r2 > c2) | (c2 >= blk), 1.0, 0.0).astype(BF16)
    zscale = (HEAD_DIM ** -0.5) * LOG2E

    def walk(blocks):
        cols = [slice(p * LANES, (p + 1) * LANES) for p in range(n_pairs)]
        starts = [pl.multiple_of(jnp.maximum(kb, 0) * blk, blk) for kb, _ in blocks]
        masks = [causal if mode == "diag" else (jnp.logical_and(causal_all, qb > 0) if mode == "prev" else None)
                 for _, mode in blocks]
        zs = []
        for cs in cols:
            qp = q_ref[0, :, cs]
            zero_q = jnp.zeros_like(qp)
            q2 = jnp.concatenate([jnp.where(lo, qp, zero_q), jnp.where(lo, zero_q, qp)], axis=0)
            zs.append([lax.dot_general(q2, k_ref[0, pl.ds(st, blk), cs], (((1,), (1,)), ((), ())),
                                       preferred_element_type=F32) for st in starts])
        log_betas, sums = [], []
        for zp in zs:
            lbs, sms = [], []
            for z, mask in zip(zp, masks):
                z = z * zscale
                log_beta = jnp.minimum(z, 0.0) - jnp.log2(1.0 + jnp.exp2(-jnp.abs(z)))
                log_1m = log_beta - z
                if mask is not None:
                    log_1m = jnp.where(mask, log_1m, 0.0)
                hi = log_1m.astype(BF16)
                rest = (log_1m - hi.astype(F32)).astype(BF16)
                sm = jnp.dot(jnp.concatenate([hi, rest], axis=0), tri_ext, preferred_element_type=F32)
                lbs.append(log_beta)
                sms.append(sm[:2 * blk] + sm[2 * blk:])
            log_betas.append(lbs)
            sums.append(sms)
        for p, cs in enumerate(cols):
            carry = carry_ref[p]
            upd = None
            for b, (st, mask) in enumerate(zip(starts, masks)):
                a = jnp.exp2(log_betas[p][b] + sums[p][b][:, :blk] + carry)
                if mask is not None:
                    a = jnp.where(mask, a, 0.0)
                carry = carry + sums[p][b][:, blk:]
                ab = a.astype(BF16)
                vs = v_ref[0, pl.ds(st, blk), cs]
                zero_v = jnp.zeros_like(vs)
                v2 = jnp.concatenate([jnp.where(lo, vs, zero_v), jnp.where(lo, zero_v, vs)], axis=0)
                part = jnp.dot(jnp.concatenate([ab[:blk], ab[blk:]], axis=1), v2, preferred_element_type=F32)
                upd = part if upd is None else upd + part
            carry_ref[p] = carry
            acc_ref[:, cs] += upd

    acc_ref[...] = jnp.zeros_like(acc_ref)
    carry_ref[...] = jnp.zeros_like(carry_ref)
    walk([(qb, "diag"), (qb - 1, "prev")])

    def cond(kb):
        return jnp.logical_and(kb >= 0, jnp.max(carry_ref[...]) > SB_EXIT_LOG2)

    def body(kb):
        walk([(kb, "full")])
        return kb - 1

    lax.while_loop(cond, body, qb - 2)
    o_ref[0] = acc_ref[...].astype(o_ref.dtype)


def _sb_attn(qkv, n_heads):
    bsz, s, _ = qkv.shape
    npair = n_heads * HEAD_DIM // LANES
    pp = min(SB_PAIRS, npair)
    ng = npair // pp
    blk = SB_BLOCK
    wide = pp * LANES
    return pl.pallas_call(
        functools.partial(_sb_kernel, n_pairs=pp),
        grid=(bsz, ng, s // blk),
        in_specs=[
            pl.BlockSpec((1, blk, wide), lambda b, g, i: (b, i, g)),
            pl.BlockSpec((1, s, wide), lambda b, g, i: (b, 0, ng + g)),
            pl.BlockSpec((1, s, wide), lambda b, g, i: (b, 0, 2 * ng + g)),
        ],
        out_specs=pl.BlockSpec((1, blk, wide), lambda b, g, i: (b, i, g)),
        out_shape=jax.ShapeDtypeStruct((bsz, s, npair * LANES), BF16),
        scratch_shapes=[pltpu.VMEM((blk, wide), F32), pltpu.VMEM((pp, 2 * blk, blk), F32)],
        compiler_params=_cparams(("parallel", "parallel", "arbitrary")),
        name="sb_attn",
    )(qkv, qkv, qkv)


def _hgrn_kernel(lbl_ref, qd_ref, fz_ref, iv_ref, gt_ref, ng_ref, sel_ref, o_ref,
                 state_ref, q_s, k_s, b_s, pcat_s, *, layer, rows, nh):
    sub = HGRN_SUB
    nsub = rows // sub
    dk = HGRN_DK
    heads = [slice(h * dk, (h + 1) * dk) for h in range(nh)]

    @pl.when(pl.program_id(2) == 0)
    def _():
        state_ref[...] = jnp.zeros_like(state_ref)

    logits = lbl_ref[...]
    ex = jnp.exp(logits - jnp.max(logits, axis=0, keepdims=True))
    sm = ex / jnp.sum(ex, axis=0, keepdims=True)
    cum = sm[0:1, :]
    for r in range(1, layer + 1):
        cum = cum + sm[r:r + 1, :]
    lb = cum - sm[0:1, :]

    fz = fz_ref[0]
    e = jnp.exp(-jnp.abs(fz))
    log_sig = jnp.minimum(fz, 0.0) - jnp.log(1.0 + e)
    la = jnp.log(jnp.maximum(lb, LB_FLOOR))
    lc = jnp.log1p(-lb) + log_sig
    logf2 = (jnp.maximum(la, lc) + jnp.log(1.0 + jnp.exp(-jnp.abs(la - lc)))) * LOG2E
    kk = (1.0 - lb) * (jnp.where(fz >= 0.0, e, 1.0) / (1.0 + e))
    qd = qd_ref[0]
    qq = qd * _sigmoid(qd)
    vv = iv_ref[0].astype(BF16)

    ri = lax.broadcasted_iota(jnp.int32, (rows, rows), 0)
    ci = lax.broadcasted_iota(jnp.int32, (rows, rows), 1)
    same = (ri // sub) == (ci // sub)
    lower = same & (ci <= ri)
    cum_mat = jnp.where(lower, 1.0, 0.0).astype(BF16)
    p0 = logf2.astype(BF16)
    r0 = logf2 - p0.astype(F32)
    p1 = r0.astype(BF16)
    p2 = (r0 - p1.astype(F32)).astype(BF16)
    b2 = (jnp.dot(cum_mat, p0, preferred_element_type=F32) + jnp.dot(cum_mat, p1, preferred_element_type=F32)
          + jnp.dot(cum_mat, p2, preferred_element_type=F32))
    for h, hs in enumerate(heads):
        k_s[h] = kk[:, hs]
        q_s[h] = qq[:, hs]
        b_s[h] = b2[:, hs]

    def diag_body(i, carry):
        base = pl.multiple_of(i * sub, sub)
        for h in range(nh):
            bi = b_s[h, pl.ds(base, sub), :]
            qi = q_s[h, pl.ds(base, sub), :]
            for s in range(sub):
                brow = b_s[h, pl.ds(base + s, 1), :]
                krow = k_s[h, pl.ds(base + s, 1), :]
                dec = jnp.exp2(jnp.minimum(bi - brow, 0.0))
                pcat_s[h, pl.ds(base, sub), s * LANES:(s + 1) * LANES] = ((qi * krow) * dec).astype(BF16)
        return carry

    lax.fori_loop(0, nsub, diag_body, 0)

    qe = (qq * jnp.exp2(b2)).astype(BF16)
    chunks = [slice(c * sub, (c + 1) * sub) for c in range(nsub)]
    lasts = [b2[(c + 1) * sub - 1:(c + 1) * sub, :] for c in range(nsub)]
    tn = (((0,), (0,)), ((), ()))
    nt = (((1,), (1,)), ((), ()))
    kvs = []
    for rs, bl in zip(chunks, lasts):
        ke = (kk[rs] * jnp.exp2(bl - b2[rs])).astype(BF16)
        kvs.append([lax.dot_general(vv[rs, hs], ke[:, hs], tn, preferred_element_type=F32) for hs in heads])
    s_reps = [jnp.dot(pcat_s[h], sel_ref[...], preferred_element_type=F32) for h in range(nh)]
    o_parts = []
    for h, hs in enumerate(heads):
        s_bd = jnp.where(lower, s_reps[h], 0.0).astype(BF16)
        o_intra = jnp.dot(s_bd, vv[:, hs], preferred_element_type=F32)
        st = state_ref[h]
        states = []
        for c in range(nsub):
            states.append(st.astype(BF16))
            st = st * jnp.exp2(lasts[c][:, hs]) + kvs[c][h]
        state_ref[h] = st
        outs = [lax.dot_general(qe[rs, hs], sb, nt, preferred_element_type=F32)
                for rs, sb in zip(chunks, states)]
        o = o_intra + jnp.concatenate(outs, axis=0)
        o_parts.append(o * lax.rsqrt(jnp.mean(o * o, axis=-1, keepdims=True) + EPS))
    o = jnp.concatenate(o_parts, axis=1) * ng_ref[...]
    gt = gt_ref[0]
    o_ref[0] = (o * (gt * _sigmoid(gt))).astype(o_ref.dtype)


def _hgrn(hg, lb_logits, norm_g, layer, n_heads):
    bsz, s, _ = hg.shape
    rows = HGRN_ROWS
    dk = HGRN_DK
    nh = HGRN_HEADS_PER_STEP if n_heads % HGRN_HEADS_PER_STEP == 0 else 1
    ng = n_heads // nh
    nl = lb_logits.shape[0]
    kk = lax.broadcasted_iota(jnp.int32, (HGRN_SUB * dk, rows), 0) // dk
    cc = lax.broadcasted_iota(jnp.int32, (HGRN_SUB * dk, rows), 1) % HGRN_SUB
    sel = (kk == cc).astype(BF16)
    blk = lambda off: pl.BlockSpec((1, rows, nh * dk), lambda b, g, r: (b, r, off * ng + g))
    return pl.pallas_call(
        functools.partial(_hgrn_kernel, layer=layer, rows=rows, nh=nh),
        grid=(bsz, ng, s // rows),
        in_specs=[
            pl.BlockSpec((nl, nh * dk), lambda b, g, r: (0, g)),
            blk(0), blk(1), blk(2), blk(3),
            pl.BlockSpec((1, nh * dk), lambda b, g, r: (0, g)),
            pl.BlockSpec((HGRN_SUB * dk, rows), lambda b, g, r: (0, 0)),
        ],
        out_specs=pl.BlockSpec((1, rows, nh * dk), lambda b, g, r: (b, r, g)),
        out_shape=jax.ShapeDtypeStruct((bsz, s, n_heads * dk), BF16),
        scratch_shapes=[
            pltpu.VMEM((nh, dk, dk), F32),
            pltpu.VMEM((nh, rows, dk), F32),
            pltpu.VMEM((nh, rows, dk), F32),
            pltpu.VMEM((nh, rows, dk), F32),
            pltpu.VMEM((nh, rows, HGRN_SUB * dk), BF16),
        ],
        compiler_params=_cparams(("parallel", "parallel", "arbitrary")),
        name="hgrn",
    )(lb_logits, hg, hg, hg, hg, norm_g, sel)


def _dup_heads(w, n_heads):
    d = w.shape[0]
    w = w.reshape(d, n_heads, 1, HEAD_DIM)
    return jnp.broadcast_to(w, (d, n_heads, 2, HEAD_DIM)).reshape(d, n_heads * 2 * HEAD_DIM)


def _pick(n, prefs):
    for c in prefs:
        if n % c == 0:
            return c
    raise ValueError(f"no tile for {n}")


def kernel(x, p, ffn_norm, ffn_w_in, ffn_w_out, mix_norm, even_w_in, even_w_out, swa_sinks, conv_w, conv_b,
           conv_ln_g, conv_ln_b, odd_w_in, odd_w_out, hgrn_lb_logits, hgrn_norm, ple_norm, ple_w_gate,
           ple_w_proj, final_norm):
    bsz, s, d = x.shape
    depth = p.shape[0]
    m = bsz * s
    mix = d // 2
    d_ff = ffn_w_out.shape[2]
    swa_heads = mix // HEAD_DIM
    swa_kv = swa_heads // 4
    sb_heads = mix // HEAD_DIM
    hgrn_heads = mix // HGRN_DK
    nq = swa_heads * HEAD_DIM
    nkv = swa_kv * HEAD_DIM

    bm = _pick(m, (1024, 512, 256))
    bm_wide = _pick(m, (512, 256))
    bn_ff = _pick(d_ff, (512, 256))

    row = lambda v: v.reshape(1, -1).astype(F32)
    xf = x.reshape(m, d)
    for i in range(depth):
        j = i // 2
        xf = _ffn(xf, row(ffn_norm[i, 0]), ffn_w_in[i, 0].astype(BF16), ffn_w_out[i, 0].astype(BF16), bm, bn_ff)
        if i % 2 == 0:
            w_in = even_w_in[j]
            w_qkv = jnp.concatenate(
                [w_in[:, :nq], _dup_heads(w_in[:, nq:nq + nkv], swa_kv),
                 _dup_heads(w_in[:, nq + nkv:nq + 2 * nkv], swa_kv)], axis=1).astype(BF16)
            qkv = _norm_proj(xf, row(mix_norm[i]), w_qkv, BF16, bm)
            a_out = _swa(qkv.reshape(bsz, s, -1), swa_sinks[j].astype(F32), swa_heads, swa_kv)
            c = _norm_glu(xf, row(mix_norm[i]), w_in[:, nq + 2 * nkv:].astype(BF16), bm)
            b_out = _conv(c.reshape(bsz, s, mix), conv_w[j].astype(F32), row(conv_b[j]), row(conv_ln_g[j]),
                          row(conv_ln_b[j]), _pick(s, (256, 128)))
            xf = _out_proj(xf, a_out.reshape(m, mix), b_out.reshape(m, mix), even_w_out[j].astype(BF16), bm)
        else:
            w_in = odd_w_in[j]
            nsb = 3 * sb_heads * HEAD_DIM
            qkv = _norm_proj(xf, row(mix_norm[i]), w_in[:, :nsb].astype(BF16), BF16, bm)
            c_out = _sb_attn(qkv.reshape(bsz, s, nsb), sb_heads)
            hg = _norm_proj(xf, row(mix_norm[i]), w_in[:, nsb:].astype(BF16), F32, bm_wide)
            d_out = _hgrn(hg.reshape(bsz, s, -1), hgrn_lb_logits.astype(F32), row(hgrn_norm[j]), j, hgrn_heads)
            xf = _out_proj(xf, c_out.reshape(m, mix), d_out.reshape(m, mix), odd_w_out[j].astype(BF16), bm)
        xf = _ffn(xf, row(ffn_norm[i, 1]), ffn_w_in[i, 1].astype(BF16), ffn_w_out[i, 1].astype(BF16), bm, bn_ff)
        xf = _ple(xf, row(ple_norm[i]), ple_w_gate[i].astype(BF16), p[i].reshape(m, -1),
                  ple_w_proj[i].astype(BF16), row(final_norm), i == depth - 1, bm)
    return xf.reshape(bsz, s, d)
```

```python
import functools

import jax
import jax.numpy as jnp
from jax import lax
from jax.experimental import pallas as pl
from jax.experimental.pallas import tpu as pltpu

F32 = jnp.float32
BF16 = jnp.bfloat16

EPS = 1e-6
NEG_BIG = -1e30
LB_FLOOR = 1e-20
HEAD_DIM = 64
WINDOW = 128
CONV_WIDTH = 31
HGRN_DK = 128
LANES = 128
SUBLANES = 8
HGRN_SUB = 16
HGRN_ROWS = 256
HGRN_HEADS_PER_STEP = 8
ROW_SUB = 256
SB_BLOCK = 128
SB_PAIRS = 8
LOG2E = 1.4426950408889634
SB_EXIT_LOG2 = -110.0 * LOG2E
VMEM_LIMIT = 58 * 1024 * 1024


def _cparams(sem):
    return pltpu.CompilerParams(dimension_semantics=sem, vmem_limit_bytes=VMEM_LIMIT)


def _rms(xf, g):
    ms = jnp.mean(xf * xf, axis=-1, keepdims=True)
    return xf * lax.rsqrt(ms + EPS) * g


def _sigmoid(a):
    return 1.0 / (1.0 + jnp.exp(-a))


def _row_tiles(n):
    return [slice(r, r + ROW_SUB) for r in range(0, n, ROW_SUB)]


def _ffn_kernel(x_ref, g_ref, wa_ref, wb_ref, wo_ref, o_ref, h_ref):
    def tile(h, base):
        a = jnp.dot(h, wa_ref[...], preferred_element_type=F32)
        b = jnp.dot(h, wb_ref[...], preferred_element_type=F32)
        gg = (0.5 * (a * _sigmoid(a)) * b).astype(BF16)
        return base + jnp.dot(gg, wo_ref[...], preferred_element_type=F32)

    def step(first):
        if first:
            for rs in _row_tiles(x_ref.shape[0]):
                xs = x_ref[rs, :]
                h = _rms(xs, g_ref[...]).astype(BF16)
                h_ref[rs, :] = h
                o_ref[rs, :] = tile(h, xs)
        else:
            o_ref[...] = tile(h_ref[...], o_ref[...])

    j = pl.program_id(1)
    pl.when(j == 0)(lambda: step(True))
    pl.when(j > 0)(lambda: step(False))


def _ffn(x, g, w_in, w_out, bm, bn):
    m, d = x.shape
    f = w_out.shape[0]
    nj = f // bn
    return pl.pallas_call(
        _ffn_kernel,
        grid=(m // bm, nj),
        in_specs=[
            pl.BlockSpec((bm, d), lambda i, j: (i, 0)),
            pl.BlockSpec((1, d), lambda i, j: (0, 0)),
            pl.BlockSpec((d, bn), lambda i, j: (0, j)),
            pl.BlockSpec((d, bn), lambda i, j: (0, j + nj)),
            pl.BlockSpec((bn, d), lambda i, j: (j, 0)),
        ],
        out_specs=pl.BlockSpec((bm, d), lambda i, j: (i, 0)),
        out_shape=jax.ShapeDtypeStruct((m, d), F32),
        scratch_shapes=[pltpu.VMEM((bm, d), BF16)],
        compiler_params=_cparams(("parallel", "arbitrary")),
        name="ffn",
    )(x, g, w_in, w_in, w_out)


def _resident(shape):
    return pl.BlockSpec(shape, lambda i: (0,) * len(shape), pipeline_mode=pl.Buffered(1))


def _norm_proj_kernel(x_ref, g_ref, w_ref, o_ref):
    for rs in _row_tiles(x_ref.shape[0]):
        h = _rms(x_ref[rs, :], g_ref[...]).astype(BF16)
        o_ref[rs, :] = jnp.dot(h, w_ref[...], preferred_element_type=F32).astype(o_ref.dtype)


def _norm_proj(x, g, w, out_dtype, bm):
    m, d = x.shape
    n = w.shape[1]
    return pl.pallas_call(
        _norm_proj_kernel,
        grid=(m // bm,),
        in_specs=[pl.BlockSpec((bm, d), lambda i: (i, 0)), _resident((1, d)), _resident((d, n))],
        out_specs=pl.BlockSpec((bm, n), lambda i: (i, 0)),
        out_shape=jax.ShapeDtypeStruct((m, n), out_dtype),
        compiler_params=_cparams(("parallel",)),
        name="norm_proj",
    )(x, g, w)


def _norm_glu_kernel(x_ref, g_ref, w_ref, o_ref):
    c = o_ref.shape[1]
    for rs in _row_tiles(x_ref.shape[0]):
        h = _rms(x_ref[rs, :], g_ref[...]).astype(BF16)
        a = jnp.dot(h, w_ref[:, :c], preferred_element_type=F32)
        b = jnp.dot(h, w_ref[:, c:], preferred_element_type=F32)
        o_ref[rs, :] = a * _sigmoid(b)


def _norm_glu(x, g, w, bm):
    m, d = x.shape
    c = w.shape[1] // 2
    return pl.pallas_call(
        _norm_glu_kernel,
        grid=(m // bm,),
        in_specs=[pl.BlockSpec((bm, d), lambda i: (i, 0)), _resident((1, d)), _resident((d, 2 * c))],
        out_specs=pl.BlockSpec((bm, c), lambda i: (i, 0)),
        out_shape=jax.ShapeDtypeStruct((m, c), F32),
        compiler_params=_cparams(("parallel",)),
        name="norm_glu",
    )(x, g, w)


def _out_proj_kernel(x_ref, a_ref, b_ref, w_ref, o_ref):
    ka = a_ref.shape[1]
    for rs in _row_tiles(x_ref.shape[0]):
        acc = jnp.dot(a_ref[rs, :], w_ref[:ka, :], preferred_element_type=F32)
        acc += jnp.dot(b_ref[rs, :], w_ref[ka:, :], preferred_element_type=F32)
        o_ref[rs, :] = x_ref[rs, :] + acc


def _out_proj(x, a, b, w, bm):
    m, d = x.shape
    ka = a.shape[1]
    assert ka == b.shape[1] and w.shape[0] == 2 * ka
    return pl.pallas_call(
        _out_proj_kernel,
        grid=(m // bm,),
        in_specs=[
            pl.BlockSpec((bm, d), lambda i: (i, 0)),
            pl.BlockSpec((bm, ka), lambda i: (i, 0)),
            pl.BlockSpec((bm, ka), lambda i: (i, 0)),
            _resident((2 * ka, d)),
        ],
        out_specs=pl.BlockSpec((bm, d), lambda i: (i, 0)),
        out_shape=jax.ShapeDtypeStruct((m, d), F32),
        compiler_params=_cparams(("parallel",)),
        name="out_proj",
    )(x, a, b, w)


def _ple_kernel(x_ref, g_ref, wg_ref, p_ref, wp_ref, gf_ref, o_ref, *, final):
    for rs in _row_tiles(x_ref.shape[0]):
        xs = x_ref[rs, :]
        h = _rms(xs, g_ref[...]).astype(BF16)
        gate = _sigmoid(jnp.dot(h, wg_ref[...], preferred_element_type=F32))
        proj = jnp.dot(p_ref[rs, :].astype(BF16), wp_ref[...], preferred_element_type=F32)
        y = xs + gate * proj
        o_ref[rs, :] = _rms(y, gf_ref[...]) if final else y


def _ple(x, g, wg, p, wp, gf, final, bm):
    m, d = x.shape
    e = p.shape[1]
    return pl.pallas_call(
        functools.partial(_ple_kernel, final=final),
        grid=(m // bm,),
        in_specs=[
            pl.BlockSpec((bm, d), lambda i: (i, 0)),
            _resident((1, d)),
            _resident((d, d)),
            pl.BlockSpec((bm, e), lambda i: (i, 0)),
            _resident((e, d)),
            _resident((1, d)),
        ],
        out_specs=pl.BlockSpec((bm, d), lambda i: (i, 0)),
        out_shape=jax.ShapeDtypeStruct((m, d), F32),
        compiler_params=_cparams(("parallel",)),
        name="ple",
    )(x, g, wg, p, wp, gf)


def _swa_kernel(sink_ref, q_ref, kc_ref, kp_ref, vc_ref, vp_ref, o_ref, *, n_pairs, group):
    w = WINDOW
    n = pl.program_id(1)
    qi = lax.broadcasted_iota(jnp.int32, (2 * w, 2 * w), 0) & (w - 1)
    sj = lax.broadcasted_iota(jnp.int32, (2 * w, 2 * w), 1)
    rel = qi + w - sj
    first_key = jnp.where(n > 0, 0, w)
    valid2 = (rel >= 0) & (rel < w) & (sj >= first_key)
    top = lax.broadcasted_iota(jnp.int32, (2 * w, 1), 0) < w
    lo = lax.broadcasted_iota(jnp.int32, (1, LANES), 1) < HEAD_DIM
    scale = HEAD_DIM ** -0.5
    nt = (((1,), (1,)), ((), ()))
    logits, vds = [], []
    for p in range(n_pairs):
        kv = (2 * p) // group
        cs = slice(kv * LANES, (kv + 1) * LANES)
        qp = q_ref[0, :, p * LANES:(p + 1) * LANES]
        zero_q = jnp.zeros_like(qp)
        q2 = jnp.concatenate([jnp.where(lo, qp, zero_q), jnp.where(lo, zero_q, qp)], axis=0)
        kd = jnp.concatenate([kp_ref[0, :, cs], kc_ref[0, :, cs]], axis=0)
        vd = jnp.concatenate([vp_ref[0, :, cs], vc_ref[0, :, cs]], axis=0)
        zero_v = jnp.zeros_like(vd)
        vds.append(jnp.concatenate([jnp.where(lo, vd, zero_v), jnp.where(lo, zero_v, vd)], axis=0))
        logits.append(lax.dot_general(q2, kd, nt, preferred_element_type=F32))
    for p in range(n_pairs):
        lg = jnp.where(valid2, logits[p] * scale, NEG_BIG)
        sink = jnp.where(top, sink_ref[2 * p], sink_ref[2 * p + 1])
        mx = jnp.maximum(jnp.max(lg, axis=-1, keepdims=True), sink)
        e = jnp.exp(lg - mx)
        probs = (e / (jnp.sum(e, axis=-1, keepdims=True) + jnp.exp(sink - mx))).astype(BF16)
        acc = jnp.dot(jnp.concatenate([probs[:w], probs[w:]], axis=1), vds[p], preferred_element_type=F32)
        o_ref[0, :, p * LANES:(p + 1) * LANES] = acc.astype(o_ref.dtype)


def _swa(qkv, sinks, n_q_heads, n_kv_heads):
    bsz, s, _ = qkv.shape
    w = WINDOW
    nq = n_q_heads * HEAD_DIM
    nk = n_kv_heads * LANES
    assert nq % nk == 0
    kblk = nq // nk
    prev = lambda b, n: jnp.maximum(n - 1, 0)
    return pl.pallas_call(
        functools.partial(_swa_kernel, n_pairs=n_q_heads // 2, group=n_q_heads // n_kv_heads),
        grid=(bsz, s // w),
        in_specs=[
            pl.BlockSpec(memory_space=pltpu.SMEM),
            pl.BlockSpec((1, w, nq), lambda b, n: (b, n, 0)),
            pl.BlockSpec((1, w, nk), lambda b, n: (b, n, kblk)),
            pl.BlockSpec((1, w, nk), lambda b, n: (b, prev(b, n), kblk)),
            pl.BlockSpec((1, w, nk), lambda b, n: (b, n, kblk + 1)),
            pl.BlockSpec((1, w, nk), lambda b, n: (b, prev(b, n), kblk + 1)),
        ],
        out_specs=pl.BlockSpec((1, w, nq), lambda b, n: (b, n, 0)),
        out_shape=jax.ShapeDtypeStruct((bsz, s, nq), BF16),
        compiler_params=_cparams(("parallel", "arbitrary")),
        name="swa",
    )(sinks, qkv, qkv, qkv, qkv, qkv)


def _conv_kernel(cur_ref, halo_ref, w_ref, b_ref, lg_ref, lb_ref, o_ref, slab_ref, sh_ref, y_ref, *,
                 bs, halo, rows, lane_group):
    i = pl.program_id(1)
    ch = cur_ref.shape[2]
    hv = halo_ref[0]
    slab_ref[0:halo, :] = jnp.where(i > 0, hv, jnp.zeros_like(hv))
    slab_ref[halo:, :] = cur_ref[0]
    off = halo - (CONV_WIDTH - 1)
    for ph in range(SUBLANES):
        sh_ref[ph, 0:bs + halo - ph, :] = slab_ref[ph:bs + halo, :]

    def taps(r, carry):
        base = pl.multiple_of(r * rows, rows)
        for g in range(ch // lane_group):
            ls = slice(g * lane_group, (g + 1) * lane_group)
            accs = [jnp.zeros((SUBLANES, lane_group), F32) + b_ref[:, ls] for _ in range(rows // SUBLANES)]
            for t in range(CONV_WIDTH):
                ph, al = (off + t) % SUBLANES, (off + t) // SUBLANES * SUBLANES
                wt = w_ref[t, :, ls]
                for u in range(rows // SUBLANES):
                    accs[u] = accs[u] + sh_ref[ph, pl.ds(base + al + u * SUBLANES, SUBLANES), ls] * wt
            for u in range(rows // SUBLANES):
                y_ref[pl.ds(base + u * SUBLANES, SUBLANES), ls] = accs[u]
        return carry

    lax.fori_loop(0, bs // rows, taps, 0)

    acc = y_ref[...]
    mu = jnp.mean(acc, axis=-1, keepdims=True)
    cen = acc - mu
    var = jnp.mean(cen * cen, axis=-1, keepdims=True)
    y = cen * lax.rsqrt(var + EPS) * lg_ref[...] + lb_ref[...]
    o_ref[0] = (y * _sigmoid(y)).astype(o_ref.dtype)


def _conv(c, conv_w, conv_b, ln_g, ln_b, bs):
    bsz, s, ch = c.shape
    halo = 32
    w_rep = jnp.broadcast_to(conv_w[:, None, :], (CONV_WIDTH, SUBLANES, ch))
    return pl.pallas_call(
        functools.partial(_conv_kernel, bs=bs, halo=halo, rows=32, lane_group=256),
        grid=(bsz, s // bs),
        in_specs=[
            pl.BlockSpec((1, bs, ch), lambda b, i: (b, i, 0)),
            pl.BlockSpec((1, halo, ch), lambda b, i: (b, jnp.maximum(i * (bs // halo) - 1, 0), 0)),
            pl.BlockSpec((CONV_WIDTH, SUBLANES, ch), lambda b, i: (0, 0, 0)),
            pl.BlockSpec((1, ch), lambda b, i: (0, 0)),
            pl.BlockSpec((1, ch), lambda b, i: (0, 0)),
            pl.BlockSpec((1, ch), lambda b, i: (0, 0)),
        ],
        out_specs=pl.BlockSpec((1, bs, ch), lambda b, i: (b, i, 0)),
        out_shape=jax.ShapeDtypeStruct((bsz, s, ch), BF16),
        scratch_shapes=[pltpu.VMEM((bs + halo, ch), F32), pltpu.VMEM((SUBLANES, bs + halo, ch), F32),
                        pltpu.VMEM((bs, ch), F32)],
        compiler_params=_cparams(("parallel", "arbitrary")),
        name="conv",
    )(c, c, w_rep, conv_b, ln_g, ln_b)


def _sb_kernel(q_ref, k_ref, v_ref, o_ref, acc_ref, carry_ref, *, n_pairs):
    blk = SB_BLOCK
    qb = pl.program_id(2)
    lo = lax.broadcasted_iota(jnp.int32, (1, LANES), 1) < HEAD_DIM
    row = lax.broadcasted_iota(jnp.int32, (2 * blk, blk), 0)
    col = lax.broadcasted_iota(jnp.int32, (2 * blk, blk), 1)
    causal = col < (row & (blk - 1))
    causal_all = col >= 0
    r2 = lax.broadcasted_iota(jnp.int32, (blk, 2 * blk), 0)
    c2 = lax.broadcasted_iota(jnp.int32, (blk, 2 * blk), 1)
    tri_ext = jnp.where((r2 > c2) | (c2 >= blk), 1.0, 0.0).astype(BF16)
    zscale = (HEAD_DIM ** -0.5) * LOG2E

    def walk(blocks):
        cols = [slice(p * LANES, (p + 1) * LANES) for p in range(n_pairs)]
        starts = [pl.multiple_of(jnp.maximum(kb, 0) * blk, blk) for kb, _ in blocks]
        masks = [causal if mode == "diag" else (jnp.logical_and(causal_all, qb > 0) if mode == "prev" else None)
                 for _, mode in blocks]
        zs = []
        for cs in cols:
            qp = q_ref[0, :, cs]
            zero_q = jnp.zeros_like(qp)
            q2 = jnp.concatenate([jnp.where(lo, qp, zero_q), jnp.where(lo, zero_q, qp)], axis=0)
            zs.append([lax.dot_general(q2, k_ref[0, pl.ds(st, blk), cs], (((1,), (1,)), ((), ())),
                                       preferred_element_type=F32) for st in starts])
        log_betas, sums = [], []
        for zp in zs:
            lbs, sms = [], []
            for z, mask in zip(zp, masks):
                z = z * zscale
                log_beta = jnp.minimum(z, 0.0) - jnp.log2(1.0 + jnp.exp2(-jnp.abs(z)))
                log_1m = log_beta - z
                if mask is not None:
                    log_1m = jnp.where(mask, log_1m, 0.0)
                hi = log_1m.astype(BF16)
                rest = (log_1m - hi.astype(F32)).astype(BF16)
                sm = jnp.dot(jnp.concatenate([hi, rest], axis=0), tri_ext, preferred_element_type=F32)
                lbs.append(log_beta)
                sms.append(sm[:2 * blk] + sm[2 * blk:])
            log_betas.append(lbs)
            sums.append(sms)
        for p, cs in enumerate(cols):
            carry = carry_ref[p]
            upd = None
            for b, (st, mask) in enumerate(zip(starts, masks)):
                a = jnp.exp2(log_betas[p][b] + sums[p][b][:, :blk] + carry)
                if mask is not None:
                    a = jnp.where(mask, a, 0.0)
                carry = carry + sums[p][b][:, blk:]
                ab = a.astype(BF16)
                vs = v_ref[0, pl.ds(st, blk), cs]
                zero_v = jnp.zeros_like(vs)
                v2 = jnp.concatenate([jnp.where(lo, vs, zero_v), jnp.where(lo, zero_v, vs)], axis=0)
                part = jnp.dot(jnp.concatenate([ab[:blk], ab[blk:]], axis=1), v2, preferred_element_type=F32)
                upd = part if upd is None else upd + part
            carry_ref[p] = carry
            acc_ref[:, cs] += upd

    acc_ref[...] = jnp.zeros_like(acc_ref)
    carry_ref[...] = jnp.zeros_like(carry_ref)
    walk([(qb, "diag"), (qb - 1, "prev")])

    def cond(kb):
        return jnp.logical_and(kb >= 0, jnp.max(carry_ref[...]) > SB_EXIT_LOG2)

    def body(kb):
        walk([(kb, "full")])
        return kb - 1

    lax.while_loop(cond, body, qb - 2)
    o_ref[0] = acc_ref[...].astype(o_ref.dtype)


def _sb_attn(qkv, n_heads):
    bsz, s, _ = qkv.shape
    npair = n_heads * HEAD_DIM // LANES
    pp = min(SB_PAIRS, npair)
    ng = npair // pp
    blk = SB_BLOCK
    wide = pp * LANES
    return pl.pallas_call(
        functools.partial(_sb_kernel, n_pairs=pp),
        grid=(bsz, ng, s // blk),
        in_specs=[
            pl.BlockSpec((1, blk, wide), lambda b, g, i: (b, i, g)),
            pl.BlockSpec((1, s, wide), lambda b, g, i: (b, 0, ng + g)),
            pl.BlockSpec((1, s, wide), lambda b, g, i: (b, 0, 2 * ng + g)),
        ],
        out_specs=pl.BlockSpec((1, blk, wide), lambda b, g, i: (b, i, g)),
        out_shape=jax.ShapeDtypeStruct((bsz, s, npair * LANES), BF16),
        scratch_shapes=[pltpu.VMEM((blk, wide), F32), pltpu.VMEM((pp, 2 * blk, blk), F32)],
        compiler_params=_cparams(("parallel", "parallel", "arbitrary")),
        name="sb_attn",
    )(qkv, qkv, qkv)


def _hgrn_kernel(lbl_ref, qd_ref, fz_ref, iv_ref, gt_ref, ng_ref, sel_ref, o_ref,
                 state_ref, q_s, k_s, b_s, pcat_s, *, layer, rows, nh):
    sub = HGRN_SUB
    nsub = rows // sub
    dk = HGRN_DK
    heads = [slice(h * dk, (h + 1) * dk) for h in range(nh)]

    @pl.when(pl.program_id(2) == 0)
    def _():
        state_ref[...] = jnp.zeros_like(state_ref)

    logits = lbl_ref[...]
    ex = jnp.exp(logits - jnp.max(logits, axis=0, keepdims=True))
    sm = ex / jnp.sum(ex, axis=0, keepdims=True)
    cum = sm[0:1, :]
    for r in range(1, layer + 1):
        cum = cum + sm[r:r + 1, :]
    lb = cum - sm[0:1, :]

    fz = fz_ref[0]
    e = jnp.exp(-jnp.abs(fz))
    log_sig = jnp.minimum(fz, 0.0) - jnp.log(1.0 + e)
    la = jnp.log(jnp.maximum(lb, LB_FLOOR))
    lc = jnp.log1p(-lb) + log_sig
    logf2 = (jnp.maximum(la, lc) + jnp.log(1.0 + jnp.exp(-jnp.abs(la - lc)))) * LOG2E
    kk = (1.0 - lb) * (jnp.where(fz >= 0.0, e, 1.0) / (1.0 + e))
    qd = qd_ref[0]
    qq = qd * _sigmoid(qd)
    vv = iv_ref[0].astype(BF16)

    ri = lax.broadcasted_iota(jnp.int32, (rows, rows), 0)
    ci = lax.broadcasted_iota(jnp.int32, (rows, rows), 1)
    same = (ri // sub) == (ci // sub)
    lower = same & (ci <= ri)
    cum_mat = jnp.where(lower, 1.0, 0.0).astype(BF16)
    p0 = logf2.astype(BF16)
    r0 = logf2 - p0.astype(F32)
    p1 = r0.astype(BF16)
    p2 = (r0 - p1.astype(F32)).astype(BF16)
    b2 = (jnp.dot(cum_mat, p0, preferred_element_type=F32) + jnp.dot(cum_mat, p1, preferred_element_type=F32)
          + jnp.dot(cum_mat, p2, preferred_element_type=F32))
    for h, hs in enumerate(heads):
        k_s[h] = kk[:, hs]
        q_s[h] = qq[:, hs]
        b_s[h] = b2[:, hs]

    for i in range(nsub):
        base = i * sub
        for h in range(nh):
            bi = b_s[h, base:base + sub, :]
            qi = q_s[h, base:base + sub, :]
            for s in range(sub):
                brow = b_s[h, base + s:base + s + 1, :]
                krow = k_s[h, base + s:base + s + 1, :]
                dec = jnp.exp2(jnp.minimum(bi - brow, 0.0))
                pcat_s[h, base:base + sub, s * LANES:(s + 1) * LANES] = ((qi * krow) * dec).astype(BF16)

    qe = (qq * jnp.exp2(b2)).astype(BF16)
    chunks = [slice(c * sub, (c + 1) * sub) for c in range(nsub)]
    lasts = [b2[(c + 1) * sub - 1:(c + 1) * sub, :] for c in range(nsub)]
    tn = (((0,), (0,)), ((), ()))
    nt = (((1,), (1,)), ((), ()))
    kvs = []
    for rs, bl in zip(chunks, lasts):
        ke = (kk[rs] * jnp.exp2(bl - b2[rs])).astype(BF16)
        kvs.append([lax.dot_general(vv[rs, hs], ke[:, hs], tn, preferred_element_type=F32) for hs in heads])
    s_reps = [jnp.dot(pcat_s[h], sel_ref[...], preferred_element_type=F32) for h in range(nh)]
    o_parts = []
    for h, hs in enumerate(heads):
        s_bd = jnp.where(lower, s_reps[h], 0.0).astype(BF16)
        o_intra = jnp.dot(s_bd, vv[:, hs], preferred_element_type=F32)
        st = state_ref[h]
        states = []
        for c in range(nsub):
            states.append(st.astype(BF16))
            st = st * jnp.exp2(lasts[c][:, hs]) + kvs[c][h]
        state_ref[h] = st
        outs = [lax.dot_general(qe[rs, hs], sb, nt, preferred_element_type=F32)
                for rs, sb in zip(chunks, states)]
        o = o_intra + jnp.concatenate(outs, axis=0)
        o_parts.append(o * lax.rsqrt(jnp.mean(o * o, axis=-1, keepdims=True) + EPS))
    o = jnp.concatenate(o_parts, axis=1) * ng_ref[...]
    gt = gt_ref[0]
    o_ref[0] = (o * (gt * _sigmoid(gt))).astype(o_ref.dtype)


def _hgrn(hg, lb_logits, norm_g, layer, n_heads):
    bsz, s, _ = hg.shape
    rows = HGRN_ROWS
    dk = HGRN_DK
    nh = HGRN_HEADS_PER_STEP if n_heads % HGRN_HEADS_PER_STEP == 0 else 1
    ng = n_heads // nh
    nl = lb_logits.shape[0]
    kk = lax.broadcasted_iota(jnp.int32, (HGRN_SUB * dk, rows), 0) // dk
    cc = lax.broadcasted_iota(jnp.int32, (HGRN_SUB * dk, rows), 1) % HGRN_SUB
    sel = (kk == cc).astype(BF16)
    blk = lambda off: pl.BlockSpec((1, rows, nh * dk), lambda b, g, r: (b, r, off * ng + g))
    return pl.pallas_call(
        functools.partial(_hgrn_kernel, layer=layer, rows=rows, nh=nh),
        grid=(bsz, ng, s // rows),
        in_specs=[
            pl.BlockSpec((nl, nh * dk), lambda b, g, r: (0, g)),
            blk(0), blk(1), blk(2), blk(3),
            pl.BlockSpec((1, nh * dk), lambda b, g, r: (0, g)),
            pl.BlockSpec((HGRN_SUB * dk, rows), lambda b, g, r: (0, 0)),
        ],
        out_specs=pl.BlockSpec((1, rows, nh * dk), lambda b, g, r: (b, r, g)),
        out_shape=jax.ShapeDtypeStruct((bsz, s, n_heads * dk), BF16),
        scratch_shapes=[
            pltpu.VMEM((nh, dk, dk), F32),
            pltpu.VMEM((nh, rows, dk), F32),
            pltpu.VMEM((nh, rows, dk), F32),
            pltpu.VMEM((nh, rows, dk), F32),
            pltpu.VMEM((nh, rows, HGRN_SUB * dk), BF16),
        ],
        compiler_params=_cparams(("parallel", "parallel", "arbitrary")),
        name="hgrn",
    )(lb_logits, hg, hg, hg, hg, norm_g, sel)


def _dup_heads(w, n_heads):
    d = w.shape[0]
    w = w.reshape(d, n_heads, 1, HEAD_DIM)
    return jnp.broadcast_to(w, (d, n_heads, 2, HEAD_DIM)).reshape(d, n_heads * 2 * HEAD_DIM)


def _pick(n, prefs):
    for c in prefs:
        if n % c == 0:
            return c
    raise ValueError(f"no tile for {n}")


def kernel(x, p, ffn_norm, ffn_w_in, ffn_w_out, mix_norm, even_w_in, even_w_out, swa_sinks, conv_w, conv_b,
           conv_ln_g, conv_ln_b, odd_w_in, odd_w_out, hgrn_lb_logits, hgrn_norm, ple_norm, ple_w_gate,
           ple_w_proj, final_norm):
    bsz, s, d = x.shape
    depth = p.shape[0]
    m = bsz * s
    mix = d // 2
    d_ff = ffn_w_out.shape[2]
    swa_heads = mix // HEAD_DIM
    swa_kv = swa_heads // 4
    sb_heads = mix // HEAD_DIM
    hgrn_heads = mix // HGRN_DK
    nq = swa_heads * HEAD_DIM
    nkv = swa_kv * HEAD_DIM

    bm = _pick(m, (1024, 512, 256))
    bm_wide = _pick(m, (512, 256))
    bn_ff = _pick(d_ff, (512, 256))

    row = lambda v: v.reshape(1, -1).astype(F32)
    xf = x.reshape(m, d)
    for i in range(depth):
        j = i // 2
        xf = _ffn(xf, row(ffn_norm[i, 0]), ffn_w_in[i, 0].astype(BF16), ffn_w_out[i, 0].astype(BF16), bm, bn_ff)
        if i % 2 == 0:
            w_in = even_w_in[j]
            w_qkv = jnp.concatenate(
                [w_in[:, :nq], _dup_heads(w_in[:, nq:nq + nkv], swa_kv),
                 _dup_heads(w_in[:, nq + nkv:nq + 2 * nkv], swa_kv)], axis=1).astype(BF16)
            qkv = _norm_proj(xf, row(mix_norm[i]), w_qkv, BF16, bm)
            a_out = _swa(qkv.reshape(bsz, s, -1), swa_sinks[j].astype(F32), swa_heads, swa_kv)
            c = _norm_glu(xf, row(mix_norm[i]), w_in[:, nq + 2 * nkv:].astype(BF16), bm)
            b_out = _conv(c.reshape(bsz, s, mix), conv_w[j].astype(F32), row(conv_b[j]), row(conv_ln_g[j]),
                          row(conv_ln_b[j]), _pick(s, (256, 128)))
            xf = _out_proj(xf, a_out.reshape(m, mix), b_out.reshape(m, mix), even_w_out[j].astype(BF16), bm)
        else:
            w_in = odd_w_in[j]
            nsb = 3 * sb_heads * HEAD_DIM
            qkv = _norm_proj(xf, row(mix_norm[i]), w_in[:, :nsb].astype(BF16), BF16, bm)
            c_out = _sb_attn(qkv.reshape(bsz, s, nsb), sb_heads)
            hg = _norm_proj(xf, row(mix_norm[i]), w_in[:, nsb:].astype(BF16), F32, bm_wide)
            d_out = _hgrn(hg.reshape(bsz, s, -1), hgrn_lb_logits.astype(F32), row(hgrn_norm[j]), j, hgrn_heads)
            xf = _out_proj(xf, c_out.reshape(m, mix), d_out.reshape(m, mix), odd_w_out[j].astype(BF16), bm)
        xf = _ffn(xf, row(ffn_norm[i, 1]), ffn_w_in[i, 1].astype(BF16), ffn_w_out[i, 1].astype(BF16), bm, bn_ff)
        xf = _ple(xf, row(ple_norm[i]), ple_w_gate[i].astype(BF16), p[i].reshape(m, -1),
                  ple_w_proj[i].astype(BF16), row(final_norm), i == depth - 1, bm)
    return xf.reshape(bsz, s, d)
```

```python
import functools

import jax
import jax.numpy as jnp
from jax import lax
from jax.experimental import pallas as pl
from jax.experimental.pallas import tpu as pltpu

F32 = jnp.float32
BF16 = jnp.bfloat16

EPS = 1e-6
NEG_BIG = -1e30
LB_FLOOR = 1e-20
HEAD_DIM = 64
WINDOW = 128
CONV_WIDTH = 31
HGRN_DK = 128
LANES = 128
SUBLANES = 8
HGRN_SUB = 16
HGRN_ROWS = 256
HGRN_HEADS_PER_STEP = 8
ROW_SUB = 256
SB_BLOCK = 128
SB_PAIRS = 8
LOG2E = 1.4426950408889634
SB_EXIT_LOG2 = -110.0 * LOG2E
VMEM_LIMIT = 58 * 1024 * 1024


def _cparams(sem):
    return pltpu.CompilerParams(dimension_semantics=sem, vmem_limit_bytes=VMEM_LIMIT)


def _rms(xf, g):
    ms = jnp.mean(xf * xf, axis=-1, keepdims=True)
    return xf * lax.rsqrt(ms + EPS) * g


def _sigmoid(a):
    return 1.0 / (1.0 + jnp.exp(-a))


def _row_tiles(n):
    return [slice(r, r + ROW_SUB) for r in range(0, n, ROW_SUB)]


def _ffn_kernel(x_ref, g_ref, wa_ref, wb_ref, wo_ref, o_ref, h_ref):
    def tile(h, base):
        a = jnp.dot(h, wa_ref[...], preferred_element_type=F32)
        b = jnp.dot(h, wb_ref[...], preferred_element_type=F32)
        gg = (0.5 * (a * _sigmoid(a)) * b).astype(BF16)
        return base + jnp.dot(gg, wo_ref[...], preferred_element_type=F32)

    def step(first):
        if first:
            for rs in _row_tiles(x_ref.shape[0]):
                xs = x_ref[rs, :]
                h = _rms(xs, g_ref[...]).astype(BF16)
                h_ref[rs, :] = h
                o_ref[rs, :] = tile(h, xs)
        else:
            o_ref[...] = tile(h_ref[...], o_ref[...])

    j = pl.program_id(1)
    pl.when(j == 0)(lambda: step(True))
    pl.when(j > 0)(lambda: step(False))


def _ffn(x, g, w_in, w_out, bm, bn):
    m, d = x.shape
    f = w_out.shape[0]
    nj = f // bn
    return pl.pallas_call(
        _ffn_kernel,
        grid=(m // bm, nj),
        in_specs=[
            pl.BlockSpec((bm, d), lambda i, j: (i, 0)),
            pl.BlockSpec((1, d), lambda i, j: (0, 0)),
            pl.BlockSpec((d, bn), lambda i, j: (0, j)),
            pl.BlockSpec((d, bn), lambda i, j: (0, j + nj)),
            pl.BlockSpec((bn, d), lambda i, j: (j, 0)),
        ],
        out_specs=pl.BlockSpec((bm, d), lambda i, j: (i, 0)),
        out_shape=jax.ShapeDtypeStruct((m, d), F32),
        scratch_shapes=[pltpu.VMEM((bm, d), BF16)],
        compiler_params=_cparams(("parallel", "arbitrary")),
        name="ffn",
    )(x, g, w_in, w_in, w_out)


def _resident(shape):
    return pl.BlockSpec(shape, lambda i: (0,) * len(shape), pipeline_mode=pl.Buffered(1))


def _norm_proj_kernel(x_ref, g_ref, w_ref, o_ref):
    for rs in _row_tiles(x_ref.shape[0]):
        h = _rms(x_ref[rs, :], g_ref[...]).astype(BF16)
        o_ref[rs, :] = jnp.dot(h, w_ref[...], preferred_element_type=F32).astype(o_ref.dtype)


def _norm_proj(x, g, w, out_dtype, bm):
    m, d = x.shape
    n = w.shape[1]
    return pl.pallas_call(
        _norm_proj_kernel,
        grid=(m // bm,),
        in_specs=[pl.BlockSpec((bm, d), lambda i: (i, 0)), _resident((1, d)), _resident((d, n))],
        out_specs=pl.BlockSpec((bm, n), lambda i: (i, 0)),
        out_shape=jax.ShapeDtypeStruct((m, n), out_dtype),
        compiler_params=_cparams(("parallel",)),
        name="norm_proj",
    )(x, g, w)


def _norm_glu_kernel(x_ref, g_ref, w_ref, o_ref):
    c = o_ref.shape[1]
    for rs in _row_tiles(x_ref.shape[0]):
        h = _rms(x_ref[rs, :], g_ref[...]).astype(BF16)
        a = jnp.dot(h, w_ref[:, :c], preferred_element_type=F32)
        b = jnp.dot(h, w_ref[:, c:], preferred_element_type=F32)
        o_ref[rs, :] = a * _sigmoid(b)


def _norm_glu(x, g, w, bm):
    m, d = x.shape
    c = w.shape[1] // 2
    return pl.pallas_call(
        _norm_glu_kernel,
        grid=(m // bm,),
        in_specs=[pl.BlockSpec((bm, d), lambda i: (i, 0)), _resident((1, d)), _resident((d, 2 * c))],
        out_specs=pl.BlockSpec((bm, c), lambda i: (i, 0)),
        out_shape=jax.ShapeDtypeStruct((m, c), F32),
        compiler_params=_cparams(("parallel",)),
        name="norm_glu",
    )(x, g, w)


def _out_proj_kernel(x_ref, a_ref, b_ref, w_ref, o_ref):
    ka = a_ref.shape[1]
    for rs in _row_tiles(x_ref.shape[0]):
        acc = jnp.dot(a_ref[rs, :], w_ref[:ka, :], preferred_element_type=F32)
        acc += jnp.dot(b_ref[rs, :], w_ref[ka:, :], preferred_element_type=F32)
        o_ref[rs, :] = x_ref[rs, :] + acc


def _out_proj(x, a, b, w, bm):
    m, d = x.shape
    ka = a.shape[1]
    assert ka == b.shape[1] and w.shape[0] == 2 * ka
    return pl.pallas_call(
        _out_proj_kernel,
        grid=(m // bm,),
        in_specs=[
            pl.BlockSpec((bm, d), lambda i: (i, 0)),
            pl.BlockSpec((bm, ka), lambda i: (i, 0)),
            pl.BlockSpec((bm, ka), lambda i: (i, 0)),
            _resident((2 * ka, d)),
        ],
        out_specs=pl.BlockSpec((bm, d), lambda i: (i, 0)),
        out_shape=jax.ShapeDtypeStruct((m, d), F32),
        compiler_params=_cparams(("parallel",)),
        name="out_proj",
    )(x, a, b, w)


def _ple_kernel(x_ref, g_ref, wg_ref, p_ref, wp_ref, gf_ref, o_ref, *, final):
    for rs in _row_tiles(x_ref.shape[0]):
        xs = x_ref[rs, :]
        h = _rms(xs, g_ref[...]).astype(BF16)
        gate = _sigmoid(jnp.dot(h, wg_ref[...], preferred_element_type=F32))
        proj = jnp.dot(p_ref[rs, :].astype(BF16), wp_ref[...], preferred_element_type=F32)
        y = xs + gate * proj
        o_ref[rs, :] = _rms(y, gf_ref[...]) if final else y


def _ple(x, g, wg, p, wp, gf, final, bm):
    m, d = x.shape
    e = p.shape[1]
    return pl.pallas_call(
        functools.partial(_ple_kernel, final=final),
        grid=(m // bm,),
        in_specs=[
            pl.BlockSpec((bm, d), lambda i: (i, 0)),
            _resident((1, d)),
            _resident((d, d)),
            pl.BlockSpec((bm, e), lambda i: (i, 0)),
            _resident((e, d)),
            _resident((1, d)),
        ],
        out_specs=pl.BlockSpec((bm, d), lambda i: (i, 0)),
        out_shape=jax.ShapeDtypeStruct((m, d), F32),
        compiler_params=_cparams(("parallel",)),
        name="ple",
    )(x, g, wg, p, wp, gf)


def _swa_kernel(sink_ref, q_ref, kc_ref, kp_ref, vc_ref, vp_ref, o_ref, *, n_pairs, group):
    w = WINDOW
    n = pl.program_id(1)
    qi = lax.broadcasted_iota(jnp.int32, (2 * w, 2 * w), 0) & (w - 1)
    sj = lax.broadcasted_iota(jnp.int32, (2 * w, 2 * w), 1)
    rel = qi + w - sj
    first_key = jnp.where(n > 0, 0, w)
    valid2 = (rel >= 0) & (rel < w) & (sj >= first_key)
    top = lax.broadcasted_iota(jnp.int32, (2 * w, 1), 0) < w
    lo = lax.broadcasted_iota(jnp.int32, (1, LANES), 1) < HEAD_DIM
    scale = HEAD_DIM ** -0.5
    nt = (((1,), (1,)), ((), ()))
    logits, vds = [], []
    for p in range(n_pairs):
        kv = (2 * p) // group
        cs = slice(kv * LANES, (kv + 1) * LANES)
        qp = q_ref[0, :, p * LANES:(p + 1) * LANES]
        zero_q = jnp.zeros_like(qp)
        q2 = jnp.concatenate([jnp.where(lo, qp, zero_q), jnp.where(lo, zero_q, qp)], axis=0)
        kd = jnp.concatenate([kp_ref[0, :, cs], kc_ref[0, :, cs]], axis=0)
        vd = jnp.concatenate([vp_ref[0, :, cs], vc_ref[0, :, cs]], axis=0)
        zero_v = jnp.zeros_like(vd)
        vds.append(jnp.concatenate([jnp.where(lo, vd, zero_v), jnp.where(lo, zero_v, vd)], axis=0))
        logits.append(lax.dot_general(q2, kd, nt, preferred_element_type=F32))
    for p in range(n_pairs):
        lg = jnp.where(valid2, logits[p] * scale, NEG_BIG)
        sink = jnp.where(top, sink_ref[2 * p], sink_ref[2 * p + 1])
        mx = jnp.maximum(jnp.max(lg, axis=-1, keepdims=True), sink)
        e = jnp.exp(lg - mx)
        probs = (e / (jnp.sum(e, axis=-1, keepdims=True) + jnp.exp(sink - mx))).astype(BF16)
        acc = jnp.dot(jnp.concatenate([probs[:w], probs[w:]], axis=1), vds[p], preferred_element_type=F32)
        o_ref[0, :, p * LANES:(p + 1) * LANES] = acc.astype(o_ref.dtype)


def _swa(qkv, sinks, n_q_heads, n_kv_heads):
    bsz, s, _ = qkv.shape
    w = WINDOW
    nq = n_q_heads * HEAD_DIM
    nk = n_kv_heads * LANES
    assert nq % nk == 0
    kblk = nq // nk
    prev = lambda b, n: jnp.maximum(n - 1, 0)
    return pl.pallas_call(
        functools.partial(_swa_kernel, n_pairs=n_q_heads // 2, group=n_q_heads // n_kv_heads),
        grid=(bsz, s // w),
        in_specs=[
            pl.BlockSpec(memory_space=pltpu.SMEM),
            pl.BlockSpec((1, w, nq), lambda b, n: (b, n, 0)),
            pl.BlockSpec((1, w, nk), lambda b, n: (b, n, kblk)),
            pl.BlockSpec((1, w, nk), lambda b, n: (b, prev(b, n), kblk)),
            pl.BlockSpec((1, w, nk), lambda b, n: (b, n, kblk + 1)),
            pl.BlockSpec((1, w, nk), lambda b, n: (b, prev(b, n), kblk + 1)),
        ],
        out_specs=pl.BlockSpec((1, w, nq), lambda b, n: (b, n, 0)),
        out_shape=jax.ShapeDtypeStruct((bsz, s, nq), BF16),
        compiler_params=_cparams(("parallel", "arbitrary")),
        name="swa",
    )(sinks, qkv, qkv, qkv, qkv, qkv)


def _conv_kernel(cur_ref, halo_ref, w_ref, b_ref, lg_ref, lb_ref, o_ref, slab_ref, sh_ref, y_ref, *,
                 bs, halo, rows, lane_group):
    i = pl.program_id(1)
    ch = cur_ref.shape[2]
    hv = halo_ref[0]
    slab_ref[0:halo, :] = jnp.where(i > 0, hv, jnp.zeros_like(hv))
    slab_ref[halo:, :] = cur_ref[0]
    off = halo - (CONV_WIDTH - 1)
    for ph in range(SUBLANES):
        sh_ref[ph, 0:bs + halo - ph, :] = slab_ref[ph:bs + halo, :]

    def taps(r, carry):
        base = pl.multiple_of(r * rows, rows)
        for g in range(ch // lane_group):
            ls = slice(g * lane_group, (g + 1) * lane_group)
            accs = [jnp.zeros((SUBLANES, lane_group), F32) + b_ref[:, ls] for _ in range(rows // SUBLANES)]
            for t in range(CONV_WIDTH):
                ph, al = (off + t) % SUBLANES, (off + t) // SUBLANES * SUBLANES
                wt = w_ref[t, :, ls]
                for u in range(rows // SUBLANES):
                    accs[u] = accs[u] + sh_ref[ph, pl.ds(base + al + u * SUBLANES, SUBLANES), ls] * wt
            for u in range(rows // SUBLANES):
                y_ref[pl.ds(base + u * SUBLANES, SUBLANES), ls] = accs[u]
        return carry

    lax.fori_loop(0, bs // rows, taps, 0)

    acc = y_ref[...]
    mu = jnp.mean(acc, axis=-1, keepdims=True)
    cen = acc - mu
    var = jnp.mean(cen * cen, axis=-1, keepdims=True)
    y = cen * lax.rsqrt(var + EPS) * lg_ref[...] + lb_ref[...]
    o_ref[0] = (y * _sigmoid(y)).astype(o_ref.dtype)


def _conv(c, conv_w, conv_b, ln_g, ln_b, bs):
    bsz, s, ch = c.shape
    halo = 32
    w_rep = jnp.broadcast_to(conv_w[:, None, :], (CONV_WIDTH, SUBLANES, ch))
    return pl.pallas_call(
        functools.partial(_conv_kernel, bs=bs, halo=halo, rows=64, lane_group=128),
        grid=(bsz, s // bs),
        in_specs=[
            pl.BlockSpec((1, bs, ch), lambda b, i: (b, i, 0)),
            pl.BlockSpec((1, halo, ch), lambda b, i: (b, jnp.maximum(i * (bs // halo) - 1, 0), 0)),
            pl.BlockSpec((CONV_WIDTH, SUBLANES, ch), lambda b, i: (0, 0, 0)),
            pl.BlockSpec((1, ch), lambda b, i: (0, 0)),
            pl.BlockSpec((1, ch), lambda b, i: (0, 0)),
            pl.BlockSpec((1, ch), lambda b, i: (0, 0)),
        ],
        out_specs=pl.BlockSpec((1, bs, ch), lambda b, i: (b, i, 0)),
        out_shape=jax.ShapeDtypeStruct((bsz, s, ch), BF16),
        scratch_shapes=[pltpu.VMEM((bs + halo, ch), F32), pltpu.VMEM((SUBLANES, bs + halo, ch), F32),
                        pltpu.VMEM((bs, ch), F32)],
        compiler_params=_cparams(("parallel", "arbitrary")),
        name="conv",
    )(c, c, w_rep, conv_b, ln_g, ln_b)


def _sb_kernel(q_ref, k_ref, v_ref, o_ref, acc_ref, carry_ref, *, n_pairs):
    blk = SB_BLOCK
    qb = pl.program_id(2)
    lo = lax.broadcasted_iota(jnp.int32, (1, LANES), 1) < HEAD_DIM
    row = lax.broadcasted_iota(jnp.int32, (2 * blk, blk), 0)
    col = lax.broadcasted_iota(jnp.int32, (2 * blk, blk), 1)
    causal = col < (row & (blk - 1))
    causal_all = col >= 0
    r2 = lax.broadcasted_iota(jnp.int32, (blk, 2 * blk), 0)
    c2 = lax.broadcasted_iota(jnp.int32, (blk, 2 * blk), 1)
    tri_ext = jnp.where((r2 > c2) | (c2 >= blk), 1.0, 0.0).astype(BF16)
    zscale = (HEAD_DIM ** -0.5) * LOG2E

    def walk(blocks):
        cols = [slice(p * LANES, (p + 1) * LANES) for p in range(n_pairs)]
        starts = [pl.multiple_of(jnp.maximum(kb, 0) * blk, blk) for kb, _ in blocks]
        masks = [causal if mode == "diag" else (jnp.logical_and(causal_all, qb > 0) if mode == "prev" else None)
                 for _, mode in blocks]
        zs = []
        for cs in cols:
            qp = q_ref[0, :, cs]
            zero_q = jnp.zeros_like(qp)
            q2 = jnp.concatenate([jnp.where(lo, qp, zero_q), jnp.where(lo, zero_q, qp)], axis=0)
            zs.append([lax.dot_general(q2, k_ref[0, pl.ds(st, blk), cs], (((1,), (1,)), ((), ())),
                                       preferred_element_type=F32) for st in starts])
        log_betas, sums = [], []
        for zp in zs:
            lbs, sms = [], []
            for z, mask in zip(zp, masks):
                z = z * zscale
                log_beta = jnp.minimum(z, 0.0) - jnp.log2(1.0 + jnp.exp2(-jnp.abs(z)))
                log_1m = log_beta - z
                if mask is not None:
                    log_1m = jnp.where(mask, log_1m, 0.0)
                hi = log_1m.astype(BF16)
                rest = (log_1m - hi.astype(F32)).astype(BF16)
                sm = jnp.dot(jnp.concatenate([hi, rest], axis=0), tri_ext, preferred_element_type=F32)
                lbs.append(log_beta)
                sms.append(sm[:2 * blk] + sm[2 * blk:])
            log_betas.append(lbs)
            sums.append(sms)
        for p, cs in enumerate(cols):
            carry = carry_ref[p]
            upd = None
            for b, (st, mask) in enumerate(zip(starts, masks)):
                a = jnp.exp2(log_betas[p][b] + sums[p][b][:, :blk] + carry)
                if mask is not None:
                    a = jnp.where(mask, a, 0.0)
                carry = carry + sums[p][b][:, blk:]
                ab = a.astype(BF16)
                vs = v_ref[0, pl.ds(st, blk), cs]
                zero_v = jnp.zeros_like(vs)
                v2 = jnp.concatenate([jnp.where(lo, vs, zero_v), jnp.where(lo, zero_v, vs)], axis=0)
                part = jnp.dot(jnp.concatenate([ab[:blk], ab[blk:]], axis=1), v2, preferred_element_type=F32)
                upd = part if upd is None else upd + part
            carry_ref[p] = carry
            acc_ref[:, cs] += upd

    acc_ref[...] = jnp.zeros_like(acc_ref)
    carry_ref[...] = jnp.zeros_like(carry_ref)
    walk([(qb, "diag"), (qb - 1, "prev")])

    def cond(kb):
        return jnp.logical_and(kb >= 0, jnp.max(carry_ref[...]) > SB_EXIT_LOG2)

    def body(kb):
        walk([(kb, "full")])
        return kb - 1

    lax.while_loop(cond, body, qb - 2)
    o_ref[0] = acc_ref[...].astype(o_ref.dtype)


def _sb_attn(qkv, n_heads):
    bsz, s, _ = qkv.shape
    npair = n_heads * HEAD_DIM // LANES
    pp = min(SB_PAIRS, npair)
    ng = npair // pp
    blk = SB_BLOCK
    wide = pp * LANES
    return pl.pallas_call(
        functools.partial(_sb_kernel, n_pairs=pp),
        grid=(bsz, ng, s // blk),
        in_specs=[
            pl.BlockSpec((1, blk, wide), lambda b, g, i: (b, i, g)),
            pl.BlockSpec((1, s, wide), lambda b, g, i: (b, 0, ng + g)),
            pl.BlockSpec((1, s, wide), lambda b, g, i: (b, 0, 2 * ng + g)),
        ],
        out_specs=pl.BlockSpec((1, blk, wide), lambda b, g, i: (b, i, g)),
        out_shape=jax.ShapeDtypeStruct((bsz, s, npair * LANES), BF16),
        scratch_shapes=[pltpu.VMEM((blk, wide), F32), pltpu.VMEM((pp, 2 * blk, blk), F32)],
        compiler_params=_cparams(("parallel", "parallel", "arbitrary")),
        name="sb_attn",
    )(qkv, qkv, qkv)


def _hgrn_kernel(lbl_ref, qd_ref, fz_ref, iv_ref, gt_ref, ng_ref, sel_ref, o_ref,
                 state_ref, q_s, k_s, b_s, pcat_s, *, layer, rows, nh):
    sub = HGRN_SUB
    nsub = rows // sub
    dk = HGRN_DK
    heads = [slice(h * dk, (h + 1) * dk) for h in range(nh)]

    @pl.when(pl.program_id(2) == 0)
    def _():
        state_ref[...] = jnp.zeros_like(state_ref)

    logits = lbl_ref[...]
    ex = jnp.exp(logits - jnp.max(logits, axis=0, keepdims=True))
    sm = ex / jnp.sum(ex, axis=0, keepdims=True)
    cum = sm[0:1, :]
    for r in range(1, layer + 1):
        cum = cum + sm[r:r + 1, :]
    lb = cum - sm[0:1, :]

    fz = fz_ref[0]
    e = jnp.exp(-jnp.abs(fz))
    log_sig = jnp.minimum(fz, 0.0) - jnp.log(1.0 + e)
    la = jnp.log(jnp.maximum(lb, LB_FLOOR))
    lc = jnp.log1p(-lb) + log_sig
    logf2 = (jnp.maximum(la, lc) + jnp.log(1.0 + jnp.exp(-jnp.abs(la - lc)))) * LOG2E
    kk = (1.0 - lb) * (jnp.where(fz >= 0.0, e, 1.0) / (1.0 + e))
    qd = qd_ref[0]
    qq = qd * _sigmoid(qd)
    vv = iv_ref[0].astype(BF16)

    ri = lax.broadcasted_iota(jnp.int32, (rows, rows), 0)
    ci = lax.broadcasted_iota(jnp.int32, (rows, rows), 1)
    same = (ri // sub) == (ci // sub)
    lower = same & (ci <= ri)
    cum_mat = jnp.where(lower, 1.0, 0.0).astype(BF16)
    p0 = logf2.astype(BF16)
    r0 = logf2 - p0.astype(F32)
    p1 = r0.astype(BF16)
    p2 = (r0 - p1.astype(F32)).astype(BF16)
    b2 = (jnp.dot(cum_mat, p0, preferred_element_type=F32) + jnp.dot(cum_mat, p1, preferred_element_type=F32)
          + jnp.dot(cum_mat, p2, preferred_element_type=F32))
    for h, hs in enumerate(heads):
        k_s[h] = kk[:, hs]
        q_s[h] = qq[:, hs]
        b_s[h] = b2[:, hs]

    for i in range(nsub):
        base = i * sub
        for h in range(nh):
            bi = b_s[h, base:base + sub, :]
            qi = q_s[h, base:base + sub, :]
            for s in range(sub):
                brow = b_s[h, base + s:base + s + 1, :]
                krow = k_s[h, base + s:base + s + 1, :]
                if s < SUBLANES:
                    prod = (qi * krow) * jnp.exp2(jnp.minimum(bi - brow, 0.0))
                else:
                    dec = jnp.exp2(jnp.minimum(bi[SUBLANES:] - brow, 0.0))
                    prod = jnp.concatenate([jnp.zeros((SUBLANES, LANES), F32), (qi[SUBLANES:] * krow) * dec], axis=0)
                pcat_s[h, base:base + sub, s * LANES:(s + 1) * LANES] = prod.astype(BF16)

    qe = (qq * jnp.exp2(b2)).astype(BF16)
    chunks = [slice(c * sub, (c + 1) * sub) for c in range(nsub)]
    lasts = [b2[(c + 1) * sub - 1:(c + 1) * sub, :] for c in range(nsub)]
    tn = (((0,), (0,)), ((), ()))
    nt = (((1,), (1,)), ((), ()))
    ke = (kk * jnp.exp2(jnp.concatenate([jnp.broadcast_to(bl, (sub, bl.shape[1])) for bl in lasts], axis=0)
                        - b2)).astype(BF16)
    s_reps = [jnp.dot(pcat_s[h], sel_ref[...], preferred_element_type=F32) for h in range(nh)]
    o_parts = []
    for h, hs in enumerate(heads):
        s_bd = jnp.where(lower, s_reps[h], 0.0).astype(BF16)
        o_intra = jnp.dot(s_bd, vv[:, hs], preferred_element_type=F32)
        st = state_ref[h]
        states = []
        for c, rs in enumerate(chunks):
            kv = lax.dot_general(vv[rs, hs], ke[rs, hs], tn, preferred_element_type=F32)
            states.append(st.astype(BF16))
            st = st * jnp.exp2(lasts[c][:, hs]) + kv
        state_ref[h] = st
        outs = [lax.dot_general(qe[rs, hs], sb, nt, preferred_element_type=F32)
                for rs, sb in zip(chunks, states)]
        o = o_intra + jnp.concatenate(outs, axis=0)
        o_parts.append(o * lax.rsqrt(jnp.mean(o * o, axis=-1, keepdims=True) + EPS))
    o = jnp.concatenate(o_parts, axis=1) * ng_ref[...]
    gt = gt_ref[0]
    o_ref[0] = (o * (gt * _sigmoid(gt))).astype(o_ref.dtype)


def _hgrn(hg, lb_logits, norm_g, layer, n_heads):
    bsz, s, _ = hg.shape
    rows = HGRN_ROWS
    dk = HGRN_DK
    nh = HGRN_HEADS_PER_STEP if n_heads % HGRN_HEADS_PER_STEP == 0 else 1
    ng = n_heads // nh
    nl = lb_logits.shape[0]
    kk = lax.broadcasted_iota(jnp.int32, (HGRN_SUB * dk, rows), 0) // dk
    cc = lax.broadcasted_iota(jnp.int32, (HGRN_SUB * dk, rows), 1) % HGRN_SUB
    sel = (kk == cc).astype(BF16)
    blk = lambda off: pl.BlockSpec((1, rows, nh * dk), lambda b, g, r: (b, r, off * ng + g))
    return pl.pallas_call(
        functools.partial(_hgrn_kernel, layer=layer, rows=rows, nh=nh),
        grid=(bsz, ng, s // rows),
        in_specs=[
            pl.BlockSpec((nl, nh * dk), lambda b, g, r: (0, g)),
            blk(0), blk(1), blk(2), blk(3),
            pl.BlockSpec((1, nh * dk), lambda b, g, r: (0, g)),
            pl.BlockSpec((HGRN_SUB * dk, rows), lambda b, g, r: (0, 0)),
        ],
        out_specs=pl.BlockSpec((1, rows, nh * dk), lambda b, g, r: (b, r, g)),
        out_shape=jax.ShapeDtypeStruct((bsz, s, n_heads * dk), BF16),
        scratch_shapes=[
            pltpu.VMEM((nh, dk, dk), F32),
            pltpu.VMEM((nh, rows, dk), F32),
            pltpu.VMEM((nh, rows, dk), F32),
            pltpu.VMEM((nh, rows, dk), F32),
            pltpu.VMEM((nh, rows, HGRN_SUB * dk), BF16),
        ],
        compiler_params=_cparams(("parallel", "parallel", "arbitrary")),
        name="hgrn",
    )(lb_logits, hg, hg, hg, hg, norm_g, sel)


def _dup_heads(w, n_heads):
    d = w.shape[0]
    w = w.reshape(d, n_heads, 1, HEAD_DIM)
    return jnp.broadcast_to(w, (d, n_heads, 2, HEAD_DIM)).reshape(d, n_heads * 2 * HEAD_DIM)


def _pick(n, prefs):
    for c in prefs:
        if n % c == 0:
            return c
    raise ValueError(f"no tile for {n}")


def kernel(x, p, ffn_norm, ffn_w_in, ffn_w_out, mix_norm, even_w_in, even_w_out, swa_sinks, conv_w, conv_b,
           conv_ln_g, conv_ln_b, odd_w_in, odd_w_out, hgrn_lb_logits, hgrn_norm, ple_norm, ple_w_gate,
           ple_w_proj, final_norm):
    bsz, s, d = x.shape
    depth = p.shape[0]
    m = bsz * s
    mix = d // 2
    d_ff = ffn_w_out.shape[2]
    swa_heads = mix // HEAD_DIM
    swa_kv = swa_heads // 4
    sb_heads = mix // HEAD_DIM
    hgrn_heads = mix // HGRN_DK
    nq = swa_heads * HEAD_DIM
    nkv = swa_kv * HEAD_DIM

    bm = _pick(m, (1024, 512, 256))
    bm_wide = _pick(m, (512, 256))
    bn_ff = _pick(d_ff, (512, 256))

    row = lambda v: v.reshape(1, -1).astype(F32)
    xf = x.reshape(m, d)
    for i in range(depth):
        j = i // 2
        xf = _ffn(xf, row(ffn_norm[i, 0]), ffn_w_in[i, 0].astype(BF16), ffn_w_out[i, 0].astype(BF16), bm, bn_ff)
        if i % 2 == 0:
            w_in = even_w_in[j]
            w_qkv = jnp.concatenate(
                [w_in[:, :nq], _dup_heads(w_in[:, nq:nq + nkv], swa_kv),
                 _dup_heads(w_in[:, nq + nkv:nq + 2 * nkv], swa_kv)], axis=1).astype(BF16)
            qkv = _norm_proj(xf, row(mix_norm[i]), w_qkv, BF16, bm)
            a_out = _swa(qkv.reshape(bsz, s, -1), swa_sinks[j].astype(F32), swa_heads, swa_kv)
            c = _norm_glu(xf, row(mix_norm[i]), w_in[:, nq + 2 * nkv:].astype(BF16), bm)
            b_out = _conv(c.reshape(bsz, s, mix), conv_w[j].astype(F32), row(conv_b[j]), row(conv_ln_g[j]),
                          row(conv_ln_b[j]), _pick(s, (256, 128)))
            xf = _out_proj(xf, a_out.reshape(m, mix), b_out.reshape(m, mix), even_w_out[j].astype(BF16), bm)
        else:
            w_in = odd_w_in[j]
            nsb = 3 * sb_heads * HEAD_DIM
            qkv = _norm_proj(xf, row(mix_norm[i]), w_in[:, :nsb].astype(BF16), BF16, bm)
            c_out = _sb_attn(qkv.reshape(bsz, s, nsb), sb_heads)
            hg = _norm_proj(xf, row(mix_norm[i]), w_in[:, nsb:].astype(BF16), F32, bm_wide)
            d_out = _hgrn(hg.reshape(bsz, s, -1), hgrn_lb_logits.astype(F32), row(hgrn_norm[j]), j, hgrn_heads)
            xf = _out_proj(xf, c_out.reshape(m, mix), d_out.reshape(m, mix), odd_w_out[j].astype(BF16), bm)
        xf = _ffn(xf, row(ffn_norm[i, 1]), ffn_w_in[i, 1].astype(BF16), ffn_w_out[i, 1].astype(BF16), bm, bn_ff)
        xf = _ple(xf, row(ple_norm[i]), ple_w_gate[i].astype(BF16), p[i].reshape(m, -1),
                  ple_w_proj[i].astype(BF16), row(final_norm), i == depth - 1, bm)
    return xf.reshape(bsz, s, d)
```

```python
import functools

import jax
import jax.numpy as jnp
from jax import lax
from jax.experimental import pallas as pl
from jax.experimental.pallas import tpu as pltpu

F32 = jnp.float32
BF16 = jnp.bfloat16

EPS = 1e-6
NEG_BIG = -1e30
LB_FLOOR = 1e-20
HEAD_DIM = 64
WINDOW = 128
CONV_WIDTH = 31
HGRN_DK = 128
LANES = 128
SUBLANES = 8
HGRN_SUB = 16
HGRN_ROWS = 256
HGRN_HEADS_PER_STEP = 8
ROW_SUB = 256
SB_BLOCK = 128
SB_PAIRS = 8
LOG2E = 1.4426950408889634
SB_EXIT_LOG2 = -110.0 * LOG2E
VMEM_LIMIT = 58 * 1024 * 1024


def _cparams(sem):
    return pltpu.CompilerParams(dimension_semantics=sem, vmem_limit_bytes=VMEM_LIMIT)


def _rms(xf, g):
    ms = jnp.mean(xf * xf, axis=-1, keepdims=True)
    return xf * lax.rsqrt(ms + EPS) * g


def _sigmoid(a):
    return 1.0 / (1.0 + jnp.exp(-a))


def _row_tiles(n):
    return [slice(r, r + ROW_SUB) for r in range(0, n, ROW_SUB)]


def _ffn_kernel(x_ref, g_ref, wa_ref, wb_ref, wo_ref, o_ref, h_ref):
    def tile(h, base):
        a = jnp.dot(h, wa_ref[...], preferred_element_type=F32)
        b = jnp.dot(h, wb_ref[...], preferred_element_type=F32)
        gg = (0.5 * (a * _sigmoid(a)) * b).astype(BF16)
        return base + jnp.dot(gg, wo_ref[...], preferred_element_type=F32)

    def step(first):
        if first:
            for rs in _row_tiles(x_ref.shape[0]):
                xs = x_ref[rs, :]
                h = _rms(xs, g_ref[...]).astype(BF16)
                h_ref[rs, :] = h
                o_ref[rs, :] = tile(h, xs)
        else:
            o_ref[...] = tile(h_ref[...], o_ref[...])

    j = pl.program_id(1)
    pl.when(j == 0)(lambda: step(True))
    pl.when(j > 0)(lambda: step(False))


def _ffn(x, g, w_in, w_out, bm, bn):
    m, d = x.shape
    f = w_out.shape[0]
    nj = f // bn
    return pl.pallas_call(
        _ffn_kernel,
        grid=(m // bm, nj),
        in_specs=[
            pl.BlockSpec((bm, d), lambda i, j: (i, 0)),
            pl.BlockSpec((1, d), lambda i, j: (0, 0)),
            pl.BlockSpec((d, bn), lambda i, j: (0, j)),
            pl.BlockSpec((d, bn), lambda i, j: (0, j + nj)),
            pl.BlockSpec((bn, d), lambda i, j: (j, 0)),
        ],
        out_specs=pl.BlockSpec((bm, d), lambda i, j: (i, 0)),
        out_shape=jax.ShapeDtypeStruct((m, d), F32),
        scratch_shapes=[pltpu.VMEM((bm, d), BF16)],
        compiler_params=_cparams(("parallel", "arbitrary")),
        name="ffn",
    )(x, g, w_in, w_in, w_out)


def _resident(shape):
    return pl.BlockSpec(shape, lambda i: (0,) * len(shape), pipeline_mode=pl.Buffered(1))


def _norm_proj_kernel(x_ref, g_ref, w_ref, o_ref):
    for rs in _row_tiles(x_ref.shape[0]):
        h = _rms(x_ref[rs, :], g_ref[...]).astype(BF16)
        o_ref[rs, :] = jnp.dot(h, w_ref[...], preferred_element_type=F32).astype(o_ref.dtype)


def _norm_proj(x, g, w, out_dtype, bm):
    m, d = x.shape
    n = w.shape[1]
    return pl.pallas_call(
        _norm_proj_kernel,
        grid=(m // bm,),
        in_specs=[pl.BlockSpec((bm, d), lambda i: (i, 0)), _resident((1, d)), _resident((d, n))],
        out_specs=pl.BlockSpec((bm, n), lambda i: (i, 0)),
        out_shape=jax.ShapeDtypeStruct((m, n), out_dtype),
        compiler_params=_cparams(("parallel",)),
        name="norm_proj",
    )(x, g, w)


def _norm_proj_glu_kernel(x_ref, g_ref, wp_ref, wg_ref, op_ref, og_ref):
    c = og_ref.shape[1]
    for rs in _row_tiles(x_ref.shape[0]):
        h = _rms(x_ref[rs, :], g_ref[...]).astype(BF16)
        op_ref[rs, :] = jnp.dot(h, wp_ref[...], preferred_element_type=F32).astype(op_ref.dtype)
        a = jnp.dot(h, wg_ref[:, :c], preferred_element_type=F32)
        b = jnp.dot(h, wg_ref[:, c:], preferred_element_type=F32)
        og_ref[rs, :] = a * _sigmoid(b)


def _norm_proj_glu(x, g, wp, wg, bm):
    m, d = x.shape
    n = wp.shape[1]
    c = wg.shape[1] // 2
    return pl.pallas_call(
        _norm_proj_glu_kernel,
        grid=(m // bm,),
        in_specs=[pl.BlockSpec((bm, d), lambda i: (i, 0)), _resident((1, d)), _resident((d, n)),
                  _resident((d, 2 * c))],
        out_specs=[pl.BlockSpec((bm, n), lambda i: (i, 0)), pl.BlockSpec((bm, c), lambda i: (i, 0))],
        out_shape=[jax.ShapeDtypeStruct((m, n), BF16), jax.ShapeDtypeStruct((m, c), F32)],
        compiler_params=_cparams(("parallel",)),
        name="norm_proj_glu",
    )(x, g, wp, wg)


def _out_proj_kernel(x_ref, a_ref, b_ref, w_ref, o_ref):
    ka = a_ref.shape[1]
    for rs in _row_tiles(x_ref.shape[0]):
        acc = jnp.dot(a_ref[rs, :], w_ref[:ka, :], preferred_element_type=F32)
        acc += jnp.dot(b_ref[rs, :], w_ref[ka:, :], preferred_element_type=F32)
        o_ref[rs, :] = x_ref[rs, :] + acc


def _out_proj(x, a, b, w, bm):
    m, d = x.shape
    ka = a.shape[1]
    assert ka == b.shape[1] and w.shape[0] == 2 * ka
    return pl.pallas_call(
        _out_proj_kernel,
        grid=(m // bm,),
        in_specs=[
            pl.BlockSpec((bm, d), lambda i: (i, 0)),
            pl.BlockSpec((bm, ka), lambda i: (i, 0)),
            pl.BlockSpec((bm, ka), lambda i: (i, 0)),
            _resident((2 * ka, d)),
        ],
        out_specs=pl.BlockSpec((bm, d), lambda i: (i, 0)),
        out_shape=jax.ShapeDtypeStruct((m, d), F32),
        compiler_params=_cparams(("parallel",)),
        name="out_proj",
    )(x, a, b, w)


def _ple_kernel(x_ref, g_ref, wg_ref, p_ref, wp_ref, gf_ref, o_ref, *, final):
    for rs in _row_tiles(x_ref.shape[0]):
        xs = x_ref[rs, :]
        h = _rms(xs, g_ref[...]).astype(BF16)
        gate = _sigmoid(jnp.dot(h, wg_ref[...], preferred_element_type=F32))
        proj = jnp.dot(p_ref[rs, :].astype(BF16), wp_ref[...], preferred_element_type=F32)
        y = xs + gate * proj
        o_ref[rs, :] = _rms(y, gf_ref[...]) if final else y


def _ple(x, g, wg, p, wp, gf, final, bm):
    m, d = x.shape
    e = p.shape[1]
    return pl.pallas_call(
        functools.partial(_ple_kernel, final=final),
        grid=(m // bm,),
        in_specs=[
            pl.BlockSpec((bm, d), lambda i: (i, 0)),
            _resident((1, d)),
            _resident((d, d)),
            pl.BlockSpec((bm, e), lambda i: (i, 0)),
            _resident((e, d)),
            _resident((1, d)),
        ],
        out_specs=pl.BlockSpec((bm, d), lambda i: (i, 0)),
        out_shape=jax.ShapeDtypeStruct((m, d), F32),
        compiler_params=_cparams(("parallel",)),
        name="ple",
    )(x, g, wg, p, wp, gf)


def _swa_kernel(sink_ref, q_ref, kc_ref, kp_ref, vc_ref, vp_ref, o_ref, *, n_pairs, group):
    w = WINDOW
    n = pl.program_id(1)
    qi = lax.broadcasted_iota(jnp.int32, (2 * w, 2 * w), 0) & (w - 1)
    sj = lax.broadcasted_iota(jnp.int32, (2 * w, 2 * w), 1)
    rel = qi + w - sj
    first_key = jnp.where(n > 0, 0, w)
    valid2 = (rel >= 0) & (rel < w) & (sj >= first_key)
    top = lax.broadcasted_iota(jnp.int32, (2 * w, 1), 0) < w
    lo = lax.broadcasted_iota(jnp.int32, (1, LANES), 1) < HEAD_DIM
    scale = HEAD_DIM ** -0.5
    nt = (((1,), (1,)), ((), ()))
    logits, vds = [], []
    for p in range(n_pairs):
        kv = (2 * p) // group
        cs = slice(kv * LANES, (kv + 1) * LANES)
        qp = q_ref[0, :, p * LANES:(p + 1) * LANES]
        zero_q = jnp.zeros_like(qp)
        q2 = jnp.concatenate([jnp.where(lo, qp, zero_q), jnp.where(lo, zero_q, qp)], axis=0)
        kd = jnp.concatenate([kp_ref[0, :, cs], kc_ref[0, :, cs]], axis=0)
        vd = jnp.concatenate([vp_ref[0, :, cs], vc_ref[0, :, cs]], axis=0)
        zero_v = jnp.zeros_like(vd)
        vds.append(jnp.concatenate([jnp.where(lo, vd, zero_v), jnp.where(lo, zero_v, vd)], axis=0))
        logits.append(lax.dot_general(q2, kd, nt, preferred_element_type=F32))
    for p in range(n_pairs):
        lg = jnp.where(valid2, logits[p] * scale, NEG_BIG)
        sink = jnp.where(top, sink_ref[2 * p], sink_ref[2 * p + 1])
        mx = jnp.maximum(jnp.max(lg, axis=-1, keepdims=True), sink)
        e = jnp.exp(lg - mx)
        probs = (e / (jnp.sum(e, axis=-1, keepdims=True) + jnp.exp(sink - mx))).astype(BF16)
        acc = jnp.dot(jnp.concatenate([probs[:w], probs[w:]], axis=1), vds[p], preferred_element_type=F32)
        o_ref[0, :, p * LANES:(p + 1) * LANES] = acc.astype(o_ref.dtype)


def _swa(qkv, sinks, n_q_heads, n_kv_heads):
    bsz, s, _ = qkv.shape
    w = WINDOW
    nq = n_q_heads * HEAD_DIM
    nk = n_kv_heads * LANES
    assert nq % nk == 0
    kblk = nq // nk
    prev = lambda b, n: jnp.maximum(n - 1, 0)
    return pl.pallas_call(
        functools.partial(_swa_kernel, n_pairs=n_q_heads // 2, group=n_q_heads // n_kv_heads),
        grid=(bsz, s // w),
        in_specs=[
            pl.BlockSpec(memory_space=pltpu.SMEM),
            pl.BlockSpec((1, w, nq), lambda b, n: (b, n, 0)),
            pl.BlockSpec((1, w, nk), lambda b, n: (b, n, kblk)),
            pl.BlockSpec((1, w, nk), lambda b, n: (b, prev(b, n), kblk)),
            pl.BlockSpec((1, w, nk), lambda b, n: (b, n, kblk + 1)),
            pl.BlockSpec((1, w, nk), lambda b, n: (b, prev(b, n), kblk + 1)),
        ],
        out_specs=pl.BlockSpec((1, w, nq), lambda b, n: (b, n, 0)),
        out_shape=jax.ShapeDtypeStruct((bsz, s, nq), BF16),
        compiler_params=_cparams(("parallel", "arbitrary")),
        name="swa",
    )(sinks, qkv, qkv, qkv, qkv, qkv)


def _conv_kernel(cur_ref, halo_ref, w_ref, b_ref, lg_ref, lb_ref, o_ref, slab_ref, sh_ref, y_ref, *,
                 bs, halo, rows, lane_group):
    i = pl.program_id(1)
    ch = cur_ref.shape[2]
    hv = halo_ref[0]
    slab_ref[0:halo, :] = jnp.where(i > 0, hv, jnp.zeros_like(hv))
    slab_ref[halo:, :] = cur_ref[0]
    off = halo - (CONV_WIDTH - 1)
    for ph in range(SUBLANES):
        sh_ref[ph, 0:bs + halo - ph, :] = slab_ref[ph:bs + halo, :]

    def taps(r, carry):
        base = pl.multiple_of(r * rows, rows)
        for g in range(ch // lane_group):
            ls = slice(g * lane_group, (g + 1) * lane_group)
            accs = [jnp.zeros((SUBLANES, lane_group), F32) + b_ref[:, ls] for _ in range(rows // SUBLANES)]
            for t in range(CONV_WIDTH):
                ph, al = (off + t) % SUBLANES, (off + t) // SUBLANES * SUBLANES
                wt = w_ref[t, :, ls]
                for u in range(rows // SUBLANES):
                    accs[u] = accs[u] + sh_ref[ph, pl.ds(base + al + u * SUBLANES, SUBLANES), ls] * wt
            for u in range(rows // SUBLANES):
                y_ref[pl.ds(base + u * SUBLANES, SUBLANES), ls] = accs[u]
        return carry

    lax.fori_loop(0, bs // rows, taps, 0)

    acc = y_ref[...]
    mu = jnp.mean(acc, axis=-1, keepdims=True)
    cen = acc - mu
    var = jnp.mean(cen * cen, axis=-1, keepdims=True)
    y = cen * lax.rsqrt(var + EPS) * lg_ref[...] + lb_ref[...]
    o_ref[0] = (y * _sigmoid(y)).astype(o_ref.dtype)


def _conv(c, conv_w, conv_b, ln_g, ln_b, bs):
    bsz, s, ch = c.shape
    halo = 32
    w_rep = jnp.broadcast_to(conv_w[:, None, :], (CONV_WIDTH, SUBLANES, ch))
    return pl.pallas_call(
        functools.partial(_conv_kernel, bs=bs, halo=halo, rows=64, lane_group=128),
        grid=(bsz, s // bs),
        in_specs=[
            pl.BlockSpec((1, bs, ch), lambda b, i: (b, i, 0)),
            pl.BlockSpec((1, halo, ch), lambda b, i: (b, jnp.maximum(i * (bs // halo) - 1, 0), 0)),
            pl.BlockSpec((CONV_WIDTH, SUBLANES, ch), lambda b, i: (0, 0, 0)),
            pl.BlockSpec((1, ch), lambda b, i: (0, 0)),
            pl.BlockSpec((1, ch), lambda b, i: (0, 0)),
            pl.BlockSpec((1, ch), lambda b, i: (0, 0)),
        ],
        out_specs=pl.BlockSpec((1, bs, ch), lambda b, i: (b, i, 0)),
        out_shape=jax.ShapeDtypeStruct((bsz, s, ch), BF16),
        scratch_shapes=[pltpu.VMEM((bs + halo, ch), F32), pltpu.VMEM((SUBLANES, bs + halo, ch), F32),
                        pltpu.VMEM((bs, ch), F32)],
        compiler_params=_cparams(("parallel", "arbitrary")),
        name="conv",
    )(c, c, w_rep, conv_b, ln_g, ln_b)


def _sb_kernel(q_ref, k_ref, v_ref, o_ref, acc_ref, carry_ref, *, n_pairs):
    blk = SB_BLOCK
    qb = pl.program_id(2)
    lo = lax.broadcasted_iota(jnp.int32, (1, LANES), 1) < HEAD_DIM
    row = lax.broadcasted_iota(jnp.int32, (2 * blk, blk), 0)
    col = lax.broadcasted_iota(jnp.int32, (2 * blk, blk), 1)
    causal = col < (row & (blk - 1))
    causal_all = col >= 0
    r2 = lax.broadcasted_iota(jnp.int32, (blk, 2 * blk), 0)
    c2 = lax.broadcasted_iota(jnp.int32, (blk, 2 * blk), 1)
    tri_ext = jnp.where((r2 > c2) | (c2 >= blk), 1.0, 0.0).astype(BF16)
    zscale = (HEAD_DIM ** -0.5) * LOG2E

    def walk(blocks):
        cols = [slice(p * LANES, (p + 1) * LANES) for p in range(n_pairs)]
        starts = [pl.multiple_of(jnp.maximum(kb, 0) * blk, blk) for kb, _ in blocks]
        masks = [causal if mode == "diag" else (jnp.logical_and(causal_all, qb > 0) if mode == "prev" else None)
                 for _, mode in blocks]
        zs = []
        for cs in cols:
            qp = q_ref[0, :, cs]
            zero_q = jnp.zeros_like(qp)
            q2 = jnp.concatenate([jnp.where(lo, qp, zero_q), jnp.where(lo, zero_q, qp)], axis=0)
            zs.append([lax.dot_general(q2, k_ref[0, pl.ds(st, blk), cs], (((1,), (1,)), ((), ())),
                                       preferred_element_type=F32) for st in starts])
        log_betas, sums = [], []
        for zp in zs:
            lbs, sms = [], []
            for z, mask in zip(zp, masks):
                z = z * zscale
                log_beta = jnp.minimum(z, 0.0) - jnp.log2(1.0 + jnp.exp2(-jnp.abs(z)))
                log_1m = log_beta - z
                if mask is not None:
                    log_1m = jnp.where(mask, log_1m, 0.0)
                hi = log_1m.astype(BF16)
                rest = (log_1m - hi.astype(F32)).astype(BF16)
                sm = jnp.dot(jnp.concatenate([hi, rest], axis=0), tri_ext, preferred_element_type=F32)
                lbs.append(log_beta)
                sms.append(sm[:2 * blk] + sm[2 * blk:])
            log_betas.append(lbs)
            sums.append(sms)
        for p, cs in enumerate(cols):
            carry = carry_ref[p]
            upd = None
            for b, (st, mask) in enumerate(zip(starts, masks)):
                a = jnp.exp2(log_betas[p][b] + sums[p][b][:, :blk] + carry)
                if mask is not None:
                    a = jnp.where(mask, a, 0.0)
                carry = carry + sums[p][b][:, blk:]
                ab = a.astype(BF16)
                vs = v_ref[0, pl.ds(st, blk), cs]
                zero_v = jnp.zeros_like(vs)
                v2 = jnp.concatenate([jnp.where(lo, vs, zero_v), jnp.where(lo, zero_v, vs)], axis=0)
                part = jnp.dot(jnp.concatenate([ab[:blk], ab[blk:]], axis=1), v2, preferred_element_type=F32)
                upd = part if upd is None else upd + part
            carry_ref[p] = carry
            acc_ref[:, cs] += upd

    acc_ref[...] = jnp.zeros_like(acc_ref)
    carry_ref[...] = jnp.zeros_like(carry_ref)
    walk([(qb, "diag"), (qb - 1, "prev")])

    def cond(kb):
        return jnp.logical_and(kb >= 0, jnp.max(carry_ref[...]) > SB_EXIT_LOG2)

    def body(kb):
        walk([(kb, "full")])
        return kb - 1

    lax.while_loop(cond, body, qb - 2)
    o_ref[0] = acc_ref[...].astype(o_ref.dtype)


def _sb_attn(qkv, n_heads):
    bsz, s, _ = qkv.shape
    npair = n_heads * HEAD_DIM // LANES
    pp = min(SB_PAIRS, npair)
    ng = npair // pp
    blk = SB_BLOCK
    wide = pp * LANES
    return pl.pallas_call(
        functools.partial(_sb_kernel, n_pairs=pp),
        grid=(bsz, ng, s // blk),
        in_specs=[
            pl.BlockSpec((1, blk, wide), lambda b, g, i: (b, i, g)),
            pl.BlockSpec((1, s, wide), lambda b, g, i: (b, 0, ng + g)),
            pl.BlockSpec((1, s, wide), lambda b, g, i: (b, 0, 2 * ng + g)),
        ],
        out_specs=pl.BlockSpec((1, blk, wide), lambda b, g, i: (b, i, g)),
        out_shape=jax.ShapeDtypeStruct((bsz, s, npair * LANES), BF16),
        scratch_shapes=[pltpu.VMEM((blk, wide), F32), pltpu.VMEM((pp, 2 * blk, blk), F32)],
        compiler_params=_cparams(("parallel", "parallel", "arbitrary")),
        name="sb_attn",
    )(qkv, qkv, qkv)


def _hgrn_kernel(lbl_ref, qd_ref, fz_ref, iv_ref, gt_ref, ng_ref, sel_ref, o_ref,
                 state_ref, q_s, k_s, b_s, pcat_s, *, layer, rows, nh):
    sub = HGRN_SUB
    nsub = rows // sub
    dk = HGRN_DK
    heads = [slice(h * dk, (h + 1) * dk) for h in range(nh)]

    @pl.when(pl.program_id(2) == 0)
    def _():
        state_ref[...] = jnp.zeros_like(state_ref)

    logits = lbl_ref[...]
    ex = jnp.exp(logits - jnp.max(logits, axis=0, keepdims=True))
    sm = ex / jnp.sum(ex, axis=0, keepdims=True)
    cum = sm[0:1, :]
    for r in range(1, layer + 1):
        cum = cum + sm[r:r + 1, :]
    lb = cum - sm[0:1, :]

    fz = fz_ref[0]
    e = jnp.exp(-jnp.abs(fz))
    log_sig = jnp.minimum(fz, 0.0) - jnp.log(1.0 + e)
    la = jnp.log(jnp.maximum(lb, LB_FLOOR))
    lc = jnp.log1p(-lb) + log_sig
    logf2 = (jnp.maximum(la, lc) + jnp.log(1.0 + jnp.exp(-jnp.abs(la - lc)))) * LOG2E
    kk = (1.0 - lb) * (jnp.where(fz >= 0.0, e, 1.0) / (1.0 + e))
    qd = qd_ref[0]
    qq = qd * _sigmoid(qd)
    vv = iv_ref[0].astype(BF16)

    ri = lax.broadcasted_iota(jnp.int32, (rows, rows), 0)
    ci = lax.broadcasted_iota(jnp.int32, (rows, rows), 1)
    same = (ri // sub) == (ci // sub)
    lower = same & (ci <= ri)
    cum_mat = jnp.where(lower, 1.0, 0.0).astype(BF16)
    p0 = logf2.astype(BF16)
    r0 = logf2 - p0.astype(F32)
    p1 = r0.astype(BF16)
    p2 = (r0 - p1.astype(F32)).astype(BF16)
    b2 = (jnp.dot(cum_mat, p0, preferred_element_type=F32) + jnp.dot(cum_mat, p1, preferred_element_type=F32)
          + jnp.dot(cum_mat, p2, preferred_element_type=F32))
    for h, hs in enumerate(heads):
        k_s[h] = kk[:, hs]
        q_s[h] = qq[:, hs]
        b_s[h] = b2[:, hs]

    for i in range(nsub):
        base = i * sub
        for h in range(nh):
            bi = b_s[h, base:base + sub, :]
            qi = q_s[h, base:base + sub, :]
            for s in range(sub):
                brow = b_s[h, base + s:base + s + 1, :]
                krow = k_s[h, base + s:base + s + 1, :]
                if s < SUBLANES:
                    prod = (qi * krow) * jnp.exp2(jnp.minimum(bi - brow, 0.0))
                else:
                    dec = jnp.exp2(jnp.minimum(bi[SUBLANES:] - brow, 0.0))
                    prod = jnp.concatenate([jnp.zeros((SUBLANES, LANES), F32), (qi[SUBLANES:] * krow) * dec], axis=0)
                pcat_s[h, base:base + sub, s * LANES:(s + 1) * LANES] = prod.astype(BF16)

    qe = (qq * jnp.exp2(b2)).astype(BF16)
    chunks = [slice(c * sub, (c + 1) * sub) for c in range(nsub)]
    lasts = [b2[(c + 1) * sub - 1:(c + 1) * sub, :] for c in range(nsub)]
    tn = (((0,), (0,)), ((), ()))
    nt = (((1,), (1,)), ((), ()))
    ke = (kk * jnp.exp2(jnp.concatenate([jnp.broadcast_to(bl, (sub, bl.shape[1])) for bl in lasts], axis=0)
                        - b2)).astype(BF16)
    s_reps = [jnp.dot(pcat_s[h], sel_ref[...], preferred_element_type=F32) for h in range(nh)]
    o_parts = []
    for h, hs in enumerate(heads):
        s_bd = jnp.where(lower, s_reps[h], 0.0).astype(BF16)
        o_intra = jnp.dot(s_bd, vv[:, hs], preferred_element_type=F32)
        st = state_ref[h]
        states = []
        for c, rs in enumerate(chunks):
            kv = lax.dot_general(vv[rs, hs], ke[rs, hs], tn, preferred_element_type=F32)
            states.append(st.astype(BF16))
            st = st * jnp.exp2(lasts[c][:, hs]) + kv
        state_ref[h] = st
        outs = [lax.dot_general(qe[rs, hs], sb, nt, preferred_element_type=F32)
                for rs, sb in zip(chunks, states)]
        o = o_intra + jnp.concatenate(outs, axis=0)
        o_parts.append(o * lax.rsqrt(jnp.mean(o * o, axis=-1, keepdims=True) + EPS))
    o = jnp.concatenate(o_parts, axis=1) * ng_ref[...]
    gt = gt_ref[0]
    o_ref[0] = (o * (gt * _sigmoid(gt))).astype(o_ref.dtype)


def _hgrn(hg, lb_logits, norm_g, layer, n_heads):
    bsz, s, _ = hg.shape
    rows = HGRN_ROWS
    dk = HGRN_DK
    nh = HGRN_HEADS_PER_STEP if n_heads % HGRN_HEADS_PER_STEP == 0 else 1
    ng = n_heads // nh
    nl = lb_logits.shape[0]
    kk = lax.broadcasted_iota(jnp.int32, (HGRN_SUB * dk, rows), 0) // dk
    cc = lax.broadcasted_iota(jnp.int32, (HGRN_SUB * dk, rows), 1) % HGRN_SUB
    sel = (kk == cc).astype(BF16)
    blk = lambda off: pl.BlockSpec((1, rows, nh * dk), lambda b, g, r: (b, r, off * ng + g))
    return pl.pallas_call(
        functools.partial(_hgrn_kernel, layer=layer, rows=rows, nh=nh),
        grid=(bsz, ng, s // rows),
        in_specs=[
            pl.BlockSpec((nl, nh * dk), lambda b, g, r: (0, g)),
            blk(0), blk(1), blk(2), blk(3),
            pl.BlockSpec((1, nh * dk), lambda b, g, r: (0, g)),
            pl.BlockSpec((HGRN_SUB * dk, rows), lambda b, g, r: (0, 0)),
        ],
        out_specs=pl.BlockSpec((1, rows, nh * dk), lambda b, g, r: (b, r, g)),
        out_shape=jax.ShapeDtypeStruct((bsz, s, n_heads * dk), BF16),
        scratch_shapes=[
            pltpu.VMEM((nh, dk, dk), F32),
            pltpu.VMEM((nh, rows, dk), F32),
            pltpu.VMEM((nh, rows, dk), F32),
            pltpu.VMEM((nh, rows, dk), F32),
            pltpu.VMEM((nh, rows, HGRN_SUB * dk), BF16),
        ],
        compiler_params=_cparams(("parallel", "parallel", "arbitrary")),
        name="hgrn",
    )(lb_logits, hg, hg, hg, hg, norm_g, sel)


def _dup_heads(w, n_heads):
    d = w.shape[0]
    w = w.reshape(d, n_heads, 1, HEAD_DIM)
    return jnp.broadcast_to(w, (d, n_heads, 2, HEAD_DIM)).reshape(d, n_heads * 2 * HEAD_DIM)


def _pick(n, prefs):
    for c in prefs:
        if n % c == 0:
            return c
    raise ValueError(f"no tile for {n}")


def kernel(x, p, ffn_norm, ffn_w_in, ffn_w_out, mix_norm, even_w_in, even_w_out, swa_sinks, conv_w, conv_b,
           conv_ln_g, conv_ln_b, odd_w_in, odd_w_out, hgrn_lb_logits, hgrn_norm, ple_norm, ple_w_gate,
           ple_w_proj, final_norm):
    bsz, s, d = x.shape
    depth = p.shape[0]
    m = bsz * s
    mix = d // 2
    d_ff = ffn_w_out.shape[2]
    swa_heads = mix // HEAD_DIM
    swa_kv = swa_heads // 4
    sb_heads = mix // HEAD_DIM
    hgrn_heads = mix // HGRN_DK
    nq = swa_heads * HEAD_DIM
    nkv = swa_kv * HEAD_DIM

    bm = _pick(m, (1024, 512, 256))
    bm_wide = _pick(m, (512, 256))
    bn_ff = _pick(d_ff, (512, 256))

    row = lambda v: v.reshape(1, -1).astype(F32)
    xf = x.reshape(m, d)
    for i in range(depth):
        j = i // 2
        xf = _ffn(xf, row(ffn_norm[i, 0]), ffn_w_in[i, 0].astype(BF16), ffn_w_out[i, 0].astype(BF16), bm, bn_ff)
        if i % 2 == 0:
            w_in = even_w_in[j]
            w_qkv = jnp.concatenate(
                [w_in[:, :nq], _dup_heads(w_in[:, nq:nq + nkv], swa_kv),
                 _dup_heads(w_in[:, nq + nkv:nq + 2 * nkv], swa_kv)], axis=1).astype(BF16)
            qkv, c = _norm_proj_glu(xf, row(mix_norm[i]), w_qkv, w_in[:, nq + 2 * nkv:].astype(BF16), bm)
            a_out = _swa(qkv.reshape(bsz, s, -1), swa_sinks[j].astype(F32), swa_heads, swa_kv)
            b_out = _conv(c.reshape(bsz, s, mix), conv_w[j].astype(F32), row(conv_b[j]), row(conv_ln_g[j]),
                          row(conv_ln_b[j]), _pick(s, (256, 128)))
            xf = _out_proj(xf, a_out.reshape(m, mix), b_out.reshape(m, mix), even_w_out[j].astype(BF16), bm)
        else:
            w_in = odd_w_in[j]
            nsb = 3 * sb_heads * HEAD_DIM
            qkv = _norm_proj(xf, row(mix_norm[i]), w_in[:, :nsb].astype(BF16), BF16, bm)
            c_out = _sb_attn(qkv.reshape(bsz, s, nsb), sb_heads)
            hg = _norm_proj(xf, row(mix_norm[i]), w_in[:, nsb:].astype(BF16), F32, bm_wide)
            d_out = _hgrn(hg.reshape(bsz, s, -1), hgrn_lb_logits.astype(F32), row(hgrn_norm[j]), j, hgrn_heads)
            xf = _out_proj(xf, c_out.reshape(m, mix), d_out.reshape(m, mix), odd_w_out[j].astype(BF16), bm)
        xf = _ffn(xf, row(ffn_norm[i, 1]), ffn_w_in[i, 1].astype(BF16), ffn_w_out[i, 1].astype(BF16), bm, bn_ff)
        xf = _ple(xf, row(ple_norm[i]), ple_w_gate[i].astype(BF16), p[i].reshape(m, -1),
                  ple_w_proj[i].astype(BF16), row(final_norm), i == depth - 1, bm)
    return xf.reshape(bsz, s, d)
```
